```python
import math
import jax
import jax.numpy as jnp
from jax import lax
import numpy as np

D_MODEL = 4096
BATCH = 32
SEQ = 256
DEPTH = 2
DEC_BATCH = 8
DEC_SEQ = 4096
PAST_LEN = 512

GRID_W = 64
Q_BLOCK = 128
ROPE_THETA = 10000.0
RMS_EPS = 1e-6
N_EVEN = (DEPTH + 1) // 2
N_ODD = DEPTH // 2

MLA_HEADS = 16
MLA_Q_RANK = 1024
MLA_KV_RANK = 512
MLA_NOPE_DIM = 128
MLA_ROPE_DIM = 64
MLA_V_DIM = 128
DIFF_HEADS = 8
DIFF_HEAD_DIM = 128
DIFF_V_DIM = 2 * DIFF_HEAD_DIM
DIFF_QK_WIDTH = DIFF_HEADS * 2 * DIFF_HEAD_DIM
GQA_Q_HEADS = 32
GQA_KV_HEADS = 8
GQA_HEAD_DIM = 128
N_EXPERTS = 32
TOP_K = 4
D_FF = 2048
SWIGLU_LIMIT = 7.0
SWIGLU_ALPHA = 1.702
MOE_BLOCK = 128

EVEN_IN_DIM = MLA_Q_RANK + MLA_KV_RANK + MLA_ROPE_DIM + 3 * DIFF_QK_WIDTH
EVEN_SPLITS = [MLA_Q_RANK,
               MLA_Q_RANK + MLA_KV_RANK,
               MLA_Q_RANK + MLA_KV_RANK + MLA_ROPE_DIM,
               MLA_Q_RANK + MLA_KV_RANK + MLA_ROPE_DIM + DIFF_QK_WIDTH,
               MLA_Q_RANK + MLA_KV_RANK + MLA_ROPE_DIM + 2 * DIFF_QK_WIDTH]
EVEN_OUT_DIM = MLA_HEADS * MLA_V_DIM + DIFF_HEADS * DIFF_V_DIM
ODD_IN_DIM = (GQA_Q_HEADS + 2 * GQA_KV_HEADS) * GQA_HEAD_DIM
ODD_SPLITS = [GQA_Q_HEADS * GQA_HEAD_DIM, (GQA_Q_HEADS + GQA_KV_HEADS) * GQA_HEAD_DIM]
ODD_OUT_DIM = GQA_Q_HEADS * GQA_HEAD_DIM

kernel_name = 'hybrid_mla_diff_gqa_moe_diffusion_step'


def rms_norm(x, gain):
    xf = x.astype(jnp.float32)
    y = xf * lax.rsqrt(jnp.mean(xf * xf, axis=-1, keepdims=True) + RMS_EPS)
    return (y * gain.astype(jnp.float32)).astype(x.dtype)


def ada_modulation(cond, w_mod, b_mod):
    m = jax.nn.silu(cond) @ w_mod + b_mod
    return [s[:, None, :] for s in jnp.split(m, 6, axis=-1)]


def axial_rope_tables(n_tokens, rot_dim):
    rows = n_tokens // GRID_W
    row = jnp.repeat(jnp.arange(rows, dtype=jnp.float32), GRID_W)
    col = jnp.tile(jnp.arange(GRID_W, dtype=jnp.float32), rows)
    axis_dim = rot_dim // 2
    inv_freq = ROPE_THETA ** (-jnp.arange(0, axis_dim, 2, dtype=jnp.float32) / axis_dim)
    ang = jnp.concatenate([row[:, None] * inv_freq, col[:, None] * inv_freq], axis=-1)
    return jnp.cos(ang), jnp.sin(ang)


def apply_rope(x, cos, sin):
    shape = (cos.shape[0],) + (1,) * (x.ndim - 3) + (cos.shape[1],)
    c = cos.reshape(shape).astype(x.dtype)
    s = sin.reshape(shape).astype(x.dtype)
    x1, x2 = x[..., 0::2], x[..., 1::2]
    return jnp.stack([x1 * c - x2 * s, x1 * s + x2 * c], axis=-1).reshape(x.shape)


def sweep_query_blocks(fn, *q_arrays):
    b, t = q_arrays[0].shape[:2]
    nb = t // Q_BLOCK
    blocked = tuple(a.reshape((b, nb, Q_BLOCK) + a.shape[2:]).swapaxes(0, 1) for a in q_arrays)
    out = lax.map(lambda qs: fn(*qs), blocked)
    return out.swapaxes(0, 1).reshape((b, t) + out.shape[3:])


def mla_attend(q_nope, q_rope, k_nope, k_rope, v):
    scale = (MLA_NOPE_DIM + MLA_ROPE_DIM) ** -0.5

    def block(qn, qr):
        s = jnp.einsum('bqhd,bkhd->bhqk', qn, k_nope) + jnp.einsum('bqhr,bkr->bhqk', qr, k_rope)
        p = jax.nn.softmax(s.astype(jnp.float32) * scale, axis=-1).astype(v.dtype)
        return jnp.einsum('bhqk,bkhd->bqhd', p, v)

    return sweep_query_blocks(block, q_nope, q_rope)


def diff_attend(q, k, v, lam):
    scale = DIFF_HEAD_DIM ** -0.5

    def block(qb):
        s = jnp.einsum('bqhmd,bkhmd->bhmqk', qb, k)
        p = jax.nn.softmax(s.astype(jnp.float32) * scale, axis=-1)
        w = (p[:, :, 0] - lam * p[:, :, 1]).astype(v.dtype)
        return jnp.einsum('bhqk,bkhd->bqhd', w, v)

    return sweep_query_blocks(block, q)


def gqa_attend(q, k, v):
    b, t = q.shape[:2]
    scale = GQA_HEAD_DIM ** -0.5
    qg = q.reshape(b, t, GQA_KV_HEADS, GQA_Q_HEADS // GQA_KV_HEADS, GQA_HEAD_DIM)

    def block(qb):
        s = jnp.einsum('bqgrd,bkgd->bgrqk', qb, k)
        p = jax.nn.softmax(s.astype(jnp.float32) * scale, axis=-1).astype(v.dtype)
        return jnp.einsum('bgrqk,bkgd->bqgrd', p, v)

    return sweep_query_blocks(block, qg).reshape(b, t, GQA_Q_HEADS * GQA_HEAD_DIM)


def diff_lambda_init(layer):
    return 0.8 - 0.6 * math.exp(-0.3 * layer)


def even_mixer(h, rope, ctx, w_in, q_norm, w_uq, kv_norm, w_ukv, lam_params, subln, w_out, lambda_init):
    b, t, _ = h.shape
    cq, ckv, krope, dq, dk, dv = jnp.split(h @ w_in, EVEN_SPLITS, axis=-1)
    q = (rms_norm(cq, q_norm) @ w_uq).reshape(b, t, MLA_HEADS, MLA_NOPE_DIM + MLA_ROPE_DIM)
    q_nope, q_rope = q[..., :MLA_NOPE_DIM], q[..., MLA_NOPE_DIM:]
    ckv = rms_norm(ckv, kv_norm)
    dq = dq.reshape(b, t, DIFF_HEADS, 2, DIFF_HEAD_DIM)
    dk = dk.reshape(b, t, DIFF_HEADS, 2, DIFF_HEAD_DIM)
    dv = dv.reshape(b, t, DIFF_HEADS, DIFF_V_DIM)
    if rope is not None:
        (cm, sm), (cd, sd) = rope[0], rope[1]
        q_rope, krope = apply_rope(q_rope, cm, sm), apply_rope(krope, cm, sm)
        dq, dk = apply_rope(dq, cd, sd), apply_rope(dk, cd, sd)
    own = (ckv, krope, dk.reshape(b, t, DIFF_HEADS, 2 * DIFF_HEAD_DIM), dv)
    if ctx is None:
        keys = own
    else:
        keys = tuple(jnp.concatenate([cx, ox], axis=1) for cx, ox in zip(ctx, own))
    ckv_k, krope_k, dk_k, dv_k = keys
    tk = ckv_k.shape[1]
    kv = (ckv_k @ w_ukv).reshape(b, tk, MLA_HEADS, MLA_NOPE_DIM + MLA_V_DIM)
    a_out = mla_attend(q_nope, q_rope, kv[..., :MLA_NOPE_DIM], krope_k, kv[..., MLA_NOPE_DIM:])
    lq1, lk1, lq2, lk2 = lam_params.astype(jnp.float32)
    lam = jnp.exp(jnp.sum(lq1 * lk1)) - jnp.exp(jnp.sum(lq2 * lk2)) + lambda_init
    b_out = diff_attend(dq, dk_k.reshape(b, tk, DIFF_HEADS, 2, DIFF_HEAD_DIM), dv_k, lam)
    b_out = rms_norm(b_out, subln) * (1.0 - lambda_init)
    merged = jnp.concatenate([a_out.reshape(b, t, -1), b_out.reshape(b, t, -1)], axis=-1)
    return merged @ w_out, own


def odd_mixer(h, rope, ctx, w_in, q_norm, k_norm, w_out):
    b, t, _ = h.shape
    q, k, v = jnp.split(h @ w_in, ODD_SPLITS, axis=-1)
    q = rms_norm(q.reshape(b, t, GQA_Q_HEADS, GQA_HEAD_DIM), q_norm)
    k = rms_norm(k.reshape(b, t, GQA_KV_HEADS, GQA_HEAD_DIM), k_norm)
    v = v.reshape(b, t, GQA_KV_HEADS, GQA_HEAD_DIM)
    if rope is not None:
        cg, sg = rope[2]
        q, k = apply_rope(q, cg, sg), apply_rope(k, cg, sg)
    own = (k, v)
    if ctx is None:
        k_k, v_k = own
    else:
        k_k = jnp.concatenate([ctx[0], k], axis=1)
        v_k = jnp.concatenate([ctx[1], v], axis=1)
    return gqa_attend(q, k_k, v_k) @ w_out, own


def moe_ffn(x, w_r, b_r, w_gu, b_gu, w_down, b_down):
    shp = x.shape
    xt = x.reshape(-1, shp[-1])
    n = xt.shape[0]
    logits = xt.astype(jnp.float32) @ w_r.astype(jnp.float32) + b_r.astype(jnp.float32)
    top_logit, top_idx = lax.top_k(logits, TOP_K)
    top_gate = jax.nn.softmax(top_logit, axis=-1)
    a = n * TOP_K
    flat_e = top_idx.reshape(a)
    order = jnp.argsort(flat_e, stable=True)
    sorted_e = flat_e[order]
    sorted_tok = (order // TOP_K).astype(jnp.int32)
    sorted_gate = top_gate.reshape(a)[order]
    counts = jnp.bincount(flat_e, length=N_EXPERTS)
    padded = (counts + MOE_BLOCK - 1) // MOE_BLOCK * MOE_BLOCK
    start = jnp.cumsum(counts) - counts
    pad_end = jnp.cumsum(padded)
    pad_start = pad_end - padded
    dest = pad_start[sorted_e] + (jnp.arange(a) - start[sorted_e])
    n_blocks = -(-a // MOE_BLOCK) + N_EXPERTS
    slots = n_blocks * MOE_BLOCK
    slot_tok = jnp.zeros((slots,), jnp.int32).at[dest].set(sorted_tok)
    slot_gate = jnp.zeros((slots,), jnp.float32).at[dest].set(sorted_gate)
    block_expert = jnp.minimum(
        jnp.searchsorted(pad_end, jnp.arange(n_blocks) * MOE_BLOCK, side='right'), N_EXPERTS - 1)

    def expert_block(args):
        tok, g, e = args
        gu = xt[tok] @ w_gu[e] + b_gu[e]
        gate_h = jnp.minimum(gu[:, :D_FF], SWIGLU_LIMIT)
        up_h = jnp.clip(gu[:, D_FF:], -SWIGLU_LIMIT, SWIGLU_LIMIT)
        hdn = (up_h + 1) * gate_h * jax.nn.sigmoid(SWIGLU_ALPHA * gate_h)
        return (hdn @ w_down[e] + b_down[e]) * g[:, None].astype(x.dtype)

    y = lax.map(expert_block, (slot_tok.reshape(n_blocks, MOE_BLOCK),
                               slot_gate.reshape(n_blocks, MOE_BLOCK), block_expert))
    out = jax.ops.segment_sum(y.reshape(slots, shp[-1]), slot_tok, num_segments=n)
    return out.reshape(shp)


def setup_inputs(seed: int = 0) -> dict:
    key = jax.random.key(seed)
    ks = iter(jax.random.split(key, 40))

    def nrm(shape, scale=1.0):
        return jax.random.normal(next(ks), shape, jnp.float32) * scale

    def gain(shape):
        return 1.0 + nrm(shape, 0.05)

    d = D_MODEL
    return {
        'x_prompt': nrm((BATCH, SEQ, d)),
        'x_sample': nrm((DEC_BATCH, DEC_SEQ, d)),
        'cache_mla_ckv': nrm((DEC_BATCH, N_EVEN, PAST_LEN, MLA_KV_RANK)),
        'cache_mla_krope': nrm((DEC_BATCH, N_EVEN, PAST_LEN, MLA_ROPE_DIM)),
        'cache_diff_k': nrm((DEC_BATCH, N_EVEN, PAST_LEN, DIFF_HEADS, 2 * DIFF_HEAD_DIM)),
        'cache_diff_v': nrm((DEC_BATCH, N_EVEN, PAST_LEN, DIFF_HEADS, DIFF_V_DIM)),
        'cache_gqa_k': nrm((DEC_BATCH, N_ODD, PAST_LEN, GQA_KV_HEADS, GQA_HEAD_DIM)),
        'cache_gqa_v': nrm((DEC_BATCH, N_ODD, PAST_LEN, GQA_KV_HEADS, GQA_HEAD_DIM)),
        'c': nrm((DEC_BATCH, d)),
        'c_ctx': nrm((d,)),
        'mod_w': nrm((DEPTH, d, 6 * d), 0.5 * d ** -0.5),
        'mod_b': nrm((DEPTH, 6 * d), 0.02),
        'norm_gains': gain((DEPTH, 4, d)),
        'even_w_in': nrm((N_EVEN, d, EVEN_IN_DIM), d ** -0.5),
        'mla_q_norm': gain((N_EVEN, MLA_Q_RANK)),
        'mla_w_uq': nrm((N_EVEN, MLA_Q_RANK, MLA_HEADS * (MLA_NOPE_DIM + MLA_ROPE_DIM)), MLA_Q_RANK ** -0.5),
        'mla_kv_norm': gain((N_EVEN, MLA_KV_RANK)),
        'mla_w_ukv': nrm((N_EVEN, MLA_KV_RANK, MLA_HEADS * (MLA_NOPE_DIM + MLA_V_DIM)), MLA_KV_RANK ** -0.5),
        'diff_lambda': nrm((N_EVEN, 4, DIFF_HEAD_DIM), 0.1),
        'diff_subln': gain((N_EVEN, DIFF_V_DIM)),
        'even_w_out': nrm((N_EVEN, EVEN_OUT_DIM, d), EVEN_OUT_DIM ** -0.5),
        'odd_w_in': nrm((N_ODD, d, ODD_IN_DIM), d ** -0.5),
        'gqa_q_norm': gain((N_ODD, GQA_HEAD_DIM)),
        'gqa_k_norm': gain((N_ODD, GQA_HEAD_DIM)),
        'odd_w_out': nrm((N_ODD, ODD_OUT_DIM, d), ODD_OUT_DIM ** -0.5),
        'router_w': nrm((DEPTH, d, N_EXPERTS), d ** -0.5),
        'router_b': nrm((DEPTH, N_EXPERTS), 0.01),
        'moe_w_gu': nrm((DEPTH, N_EXPERTS, d, 2 * D_FF), d ** -0.5),
        'moe_b_gu': nrm((DEPTH, N_EXPERTS, 2 * D_FF), 0.02),
        'moe_w_down': nrm((DEPTH, N_EXPERTS, D_FF, d), D_FF ** -0.5),
        'moe_b_down': nrm((DEPTH, N_EXPERTS, d), 0.02),
    }


def reference(x_prompt, x_sample, cache_mla_ckv, cache_mla_krope, cache_diff_k, cache_diff_v,
              cache_gqa_k, cache_gqa_v, c, c_ctx, mod_w, mod_b, norm_gains,
              even_w_in, mla_q_norm, mla_w_uq, mla_kv_norm, mla_w_ukv, diff_lambda, diff_subln,
              even_w_out, odd_w_in, gqa_q_norm, gqa_k_norm, odd_w_out,
              router_w, router_b, moe_w_gu, moe_b_gu, moe_w_down, moe_b_down):

    def run_layer(layer, x, cond, rope, ctx):
        sh_m, sc_m, g_m, sh_f, sc_f, g_f = ada_modulation(cond, mod_w[layer], mod_b[layer])
        gains = norm_gains[layer]
        h = rms_norm(x, gains[0]) * (1 + sc_m) + sh_m
        i = layer // 2
        if layer % 2 == 0:
            m, own = even_mixer(h, rope, ctx, even_w_in[i], mla_q_norm[i], mla_w_uq[i], mla_kv_norm[i],
                                mla_w_ukv[i], diff_lambda[i], diff_subln[i], even_w_out[i],
                                diff_lambda_init(layer))
        else:
            m, own = odd_mixer(h, rope, ctx, odd_w_in[i], gqa_q_norm[i], gqa_k_norm[i], odd_w_out[i])
        x = x + g_m * rms_norm(m, gains[1])
        h = rms_norm(x, gains[2]) * (1 + sc_f) + sh_f
        f = moe_ffn(h, router_w[layer], router_b[layer], moe_w_gu[layer], moe_b_gu[layer],
                    moe_w_down[layer], moe_b_down[layer])
        return x + g_f * rms_norm(f, gains[3]), own

    ctx_cond = c_ctx[None, :]
    x = x_prompt
    even_states, odd_states = [], []
    for layer in range(DEPTH):
        x, own = run_layer(layer, x, ctx_cond, None, None)
        if layer % 2 == 0:
            even_states.append(own)
        else:
            odd_states.append(own)
    y_prompt = x
    new_mla_ckv = jnp.stack([s[0] for s in even_states], axis=1)
    new_mla_krope = jnp.stack([s[1] for s in even_states], axis=1)
    new_diff_k = jnp.stack([s[2] for s in even_states], axis=1)
    new_diff_v = jnp.stack([s[3] for s in even_states], axis=1)
    new_gqa_k = jnp.stack([s[0] for s in odd_states], axis=1)
    new_gqa_v = jnp.stack([s[1] for s in odd_states], axis=1)

    n_lat = x_sample.shape[1]
    rope = (axial_rope_tables(n_lat, MLA_ROPE_DIM),
            axial_rope_tables(n_lat, DIFF_HEAD_DIM),
            axial_rope_tables(n_lat, GQA_HEAD_DIM))
    x = x_sample
    for layer in range(DEPTH):
        i = layer // 2
        if layer % 2 == 0:
            ctx = (cache_mla_ckv[:, i], cache_mla_krope[:, i], cache_diff_k[:, i], cache_diff_v[:, i])
        else:
            ctx = (cache_gqa_k[:, i], cache_gqa_v[:, i])
        x, _ = run_layer(layer, x, c, rope, ctx)
    y_sample = x

    return (y_prompt, y_sample, new_mla_ckv, new_mla_krope, new_diff_k, new_diff_v, new_gqa_k, new_gqa_v)
```

```python
import functools
import math

import jax
import jax.numpy as jnp
from jax import lax
from jax.experimental import pallas as pl
from jax.experimental.pallas import tpu as pltpu

F32 = jnp.float32
BF16 = jnp.bfloat16

GRID_W = 64
ROPE_THETA = 10000.0
RMS_EPS = 1e-6
MLA_HEADS = 16
MLA_Q_RANK = 1024
MLA_KV_RANK = 512
MLA_NOPE_DIM = 128
MLA_ROPE_DIM = 64
MLA_V_DIM = 128
DIFF_HEADS = 8
DIFF_HEAD_DIM = 128
DIFF_V_DIM = 2 * DIFF_HEAD_DIM
DIFF_QK_WIDTH = DIFF_HEADS * 2 * DIFF_HEAD_DIM
GQA_Q_HEADS = 32
GQA_KV_HEADS = 8
GQA_HEAD_DIM = 128
N_EXPERTS = 32
TOP_K = 4
SWIGLU_LIMIT = 7.0
SWIGLU_ALPHA = 1.702

LANE = 128
VMEM_LIMIT = 56 * 1024 * 1024


def _tile(dim, pref):
    t = min(pref, dim)
    while dim % t:
        t //= 2
    return t


def _params(sem):
    return pltpu.CompilerParams(dimension_semantics=sem, vmem_limit_bytes=VMEM_LIMIT)


def _mm_kernel(x_ref, w_ref, o_ref):
    o_ref[...] = jnp.dot(x_ref[...].astype(BF16), w_ref[...].astype(BF16),
                         preferred_element_type=F32).astype(o_ref.dtype)


def mm(x, w, out_dtype=F32, tm=1024, tn=512):
    m, k = x.shape
    n = w.shape[1]
    tm = _tile(m, tm)
    tn = _tile(n, tn)
    return pl.pallas_call(
        _mm_kernel,
        grid=(m // tm, n // tn),
        in_specs=[pl.BlockSpec((tm, k), lambda i, j: (i, 0)),
                  pl.BlockSpec((k, tn), lambda i, j: (0, j))],
        out_specs=pl.BlockSpec((tm, tn), lambda i, j: (i, j)),
        out_shape=jax.ShapeDtypeStruct((m, n), out_dtype),
        compiler_params=_params(("parallel", "parallel")),
        name="mm",
    )(x, w)


def _seg_of_block(i, tm, n_prompt, rows_per_sample):
    pb = n_prompt // tm
    return jnp.where(i < pb, 0, 1 + (i - pb) // (rows_per_sample // tm))


def _norm_mod_kernel(x_ref, gain_ref, sc_ref, sh_ref, o_ref):
    x = x_ref[...]
    y = x * lax.rsqrt(jnp.mean(x * x, axis=-1, keepdims=True) + RMS_EPS) * gain_ref[...]
    o_ref[...] = (y * (1.0 + sc_ref[...]) + sh_ref[...]).astype(o_ref.dtype)


def norm_mod(x, gain, scale, shift, n_prompt, rows_per_sample, out_dtype):
    n, d = x.shape
    tm = _tile(math.gcd(n_prompt, rows_per_sample), 256)
    seg = functools.partial(_seg_of_block, tm=tm, n_prompt=n_prompt, rows_per_sample=rows_per_sample)
    s = scale.shape[0]
    return pl.pallas_call(
        _norm_mod_kernel,
        grid=(n // tm,),
        in_specs=[pl.BlockSpec((tm, d), lambda i: (i, 0)),
                  pl.BlockSpec((1, d), lambda i: (0, 0)),
                  pl.BlockSpec((None, 1, d), lambda i: (seg(i), 0, 0)),
                  pl.BlockSpec((None, 1, d), lambda i: (seg(i), 0, 0))],
        out_specs=pl.BlockSpec((tm, d), lambda i: (i, 0)),
        out_shape=jax.ShapeDtypeStruct((n, d), out_dtype),
        compiler_params=_params(("parallel",)),
        name="norm_mod",
    )(x, gain.reshape(1, d), scale.reshape(s, 1, d), shift.reshape(s, 1, d))


def _resid_kernel(x_ref, m_ref, gain_ref, g_ref, o_ref):
    m = m_ref[...]
    y = m * lax.rsqrt(jnp.mean(m * m, axis=-1, keepdims=True) + RMS_EPS) * gain_ref[...]
    o_ref[...] = x_ref[...] + g_ref[...] * y


def resid(x, m, gain, gate, n_prompt, rows_per_sample):
    n, d = x.shape
    tm = _tile(math.gcd(n_prompt, rows_per_sample), 256)
    seg = functools.partial(_seg_of_block, tm=tm, n_prompt=n_prompt, rows_per_sample=rows_per_sample)
    s = gate.shape[0]
    return pl.pallas_call(
        _resid_kernel,
        grid=(n // tm,),
        in_specs=[pl.BlockSpec((tm, d), lambda i: (i, 0)),
                  pl.BlockSpec((tm, d), lambda i: (i, 0)),
                  pl.BlockSpec((1, d), lambda i: (0, 0)),
                  pl.BlockSpec((None, 1, d), lambda i: (seg(i), 0, 0))],
        out_specs=pl.BlockSpec((tm, d), lambda i: (i, 0)),
        out_shape=jax.ShapeDtypeStruct((n, d), F32),
        compiler_params=_params(("parallel",)),
        name="resid",
    )(x, m, gain.reshape(1, d), gate.reshape(s, 1, d))


def _attn_kernel(*refs, n_q, n_k, tkc, n_chunks, scale):
    q_refs = refs[:n_q]
    k_refs = refs[n_q:n_q + n_k]
    v_ref = refs[n_q + n_k]
    o_ref = refs[n_q + n_k + 1]
    m_sc, l_sc, acc_sc = refs[n_q + n_k + 2:]

    q = jnp.concatenate([r[...] for r in q_refs], axis=1) if n_q > 1 else q_refs[0][...]
    m_sc[...] = jnp.full(m_sc.shape, -jnp.inf, F32)
    l_sc[...] = jnp.zeros(l_sc.shape, F32)
    acc_sc[...] = jnp.zeros(acc_sc.shape, F32)

    def chunk(c, carry):
        rows = pl.ds(pl.multiple_of(c * tkc, tkc), tkc)
        ks = [r[rows, :] for r in k_refs]
        k = jnp.concatenate(ks, axis=1) if n_k > 1 else ks[0]
        s = lax.dot_general(q, k, (((1,), (1,)), ((), ())), preferred_element_type=F32) * scale
        m_prev = m_sc[...]
        m_new = jnp.maximum(m_prev, jnp.max(s, axis=1, keepdims=True))
        alpha = jnp.exp(m_prev - m_new)
        p = jnp.exp(s - m_new)
        l_sc[...] = alpha * l_sc[...] + jnp.sum(p, axis=1, keepdims=True)
        acc_sc[...] = alpha * acc_sc[...] + jnp.dot(p.astype(BF16), v_ref[rows, :],
                                                    preferred_element_type=F32)
        m_sc[...] = m_new
        return carry

    lax.fori_loop(0, n_chunks, chunk, 0)
    o_ref[...] = (acc_sc[...] / l_sc[...]).astype(o_ref.dtype)


def attention(q_parts, k_parts, v_part, *, batch, heads, tq_total, tk_total, q_row0, k_row0,
              dv, scale, out_dtype):
    tq = _tile(tq_total, 1024)
    tkc = _tile(tk_total, 512)
    nq = tq_total // tq
    assert q_row0 % tq == 0 and k_row0 % tk_total == 0
    qb0, kb0 = q_row0 // tq, k_row0 // tk_total

    def q_spec(fn):
        return pl.BlockSpec((tq, LANE), lambda b, h, i: (qb0 + b * nq + i, fn(h)))

    def k_spec(fn, width):
        return pl.BlockSpec((tk_total, width), lambda b, h, i: (kb0 + b, fn(h)))

    in_specs = ([q_spec(fn) for _, fn in q_parts] + [k_spec(fn, LANE) for _, fn in k_parts]
                + [k_spec(v_part[1], dv)])
    kern = functools.partial(_attn_kernel, n_q=len(q_parts), n_k=len(k_parts), tkc=tkc,
                             n_chunks=tk_total // tkc, scale=scale)
    return pl.pallas_call(
        kern,
        grid=(batch, heads, nq),
        in_specs=in_specs,
        out_specs=pl.BlockSpec((tq, dv), lambda b, h, i: (b * nq + i, h)),
        out_shape=jax.ShapeDtypeStruct((batch * tq_total, heads * dv), out_dtype),
        scratch_shapes=[pltpu.VMEM((tq, 1), F32), pltpu.VMEM((tq, 1), F32), pltpu.VMEM((tq, dv), F32)],
        compiler_params=_params(("parallel", "parallel", "parallel")),
        name="attention",
    )(*[a for a, _ in q_parts], *[a for a, _ in k_parts], v_part[0])


MOE_TM = 512
MOE_GATHER_TM = 256
MOE_COMBINE_TM = 128


def _moe_gather_kernel(tok_ref, nused_ref, h_ref, o_ref, buf, sem, *, tm, blocks_per_moe_block):
    b = pl.program_id(0)
    used = b < nused_ref[0] * blocks_per_moe_block

    @pl.when(used)
    def _():
        def issue(r, carry):
            pltpu.make_async_copy(h_ref.at[pl.ds(tok_ref[r], 1), :], buf.at[pl.ds(r, 1), :], sem).start()
            return carry
        lax.fori_loop(0, tm, issue, 0)
        pltpu.make_async_copy(h_ref.at[pl.ds(0, tm), :], buf, sem).wait()
        o_ref[...] = buf[...].astype(o_ref.dtype)

    @pl.when(jnp.logical_not(used))
    def _():
        o_ref[...] = jnp.zeros(o_ref.shape, o_ref.dtype)


def moe_gather(h, slot_tok, nused, n_slots):
    n, d = h.shape
    tm = MOE_GATHER_TM
    nb = n_slots // tm
    kern = functools.partial(_moe_gather_kernel, tm=tm, blocks_per_moe_block=MOE_TM // tm)
    return pl.pallas_call(
        kern,
        grid=(nb,),
        in_specs=[pl.BlockSpec((None, None, tm), lambda b: (b, 0, 0), memory_space=pltpu.SMEM),
                  pl.BlockSpec(memory_space=pltpu.SMEM),
                  pl.BlockSpec(memory_space=pl.ANY)],
        out_specs=pl.BlockSpec((tm, d), lambda b: (b, 0)),
        out_shape=jax.ShapeDtypeStruct((n_slots, d), BF16),
        scratch_shapes=[pltpu.VMEM((tm, d), F32), pltpu.SemaphoreType.DMA(())],
        compiler_params=_params(("arbitrary",)),
        name="moe_gather",
    )(slot_tok.reshape(nb, 1, tm), nused, h)


def _expert_changed(be_ref, b):
    return jnp.logical_or(b == 0, be_ref[b] != be_ref[jnp.maximum(b - 1, 0)])


def _moe_gu_kernel(be_ref, nused_ref, x_ref, wg_ref, wu_ref, bg_ref, bu_ref, o_ref, wg_bf, wu_bf):
    b = pl.program_id(1)

    @pl.when(_expert_changed(be_ref, b))
    def _():
        wg_bf[...] = wg_ref[...].astype(BF16)
        wu_bf[...] = wu_ref[...].astype(BF16)

    @pl.when(b < nused_ref[0])
    def _():
        x = x_ref[...]
        g = jnp.dot(x, wg_bf[...], preferred_element_type=F32) + bg_ref[...]
        u = jnp.dot(x, wu_bf[...], preferred_element_type=F32) + bu_ref[...]
        gate_h = jnp.minimum(g, SWIGLU_LIMIT)
        up_h = jnp.clip(u, -SWIGLU_LIMIT, SWIGLU_LIMIT)
        o_ref[...] = ((up_h + 1.0) * gate_h * jax.nn.sigmoid(SWIGLU_ALPHA * gate_h)).astype(o_ref.dtype)

    @pl.when(b >= nused_ref[0])
    def _():
        o_ref[...] = jnp.zeros(o_ref.shape, o_ref.dtype)


def moe_gate_up(xs, w_gu, b_gu, block_expert, nused):
    n_slots, d = xs.shape
    n_e, _, two_ff = w_gu.shape
    d_ff = two_ff // 2
    tm = MOE_TM
    tn = _tile(d_ff, 256)
    nj = d_ff // tn
    nb = n_slots // tm
    grid_spec = pltpu.PrefetchScalarGridSpec(
        num_scalar_prefetch=2,
        grid=(nj, nb),
        in_specs=[pl.BlockSpec((tm, d), lambda j, b, be, nu: (b, 0)),
                  pl.BlockSpec((None, d, tn), lambda j, b, be, nu: (be[b], 0, j)),
                  pl.BlockSpec((None, d, tn), lambda j, b, be, nu: (be[b], 0, nj + j)),
                  pl.BlockSpec((None, 1, tn), lambda j, b, be, nu: (be[b], 0, j)),
                  pl.BlockSpec((None, 1, tn), lambda j, b, be, nu: (be[b], 0, nj + j))],
        out_specs=pl.BlockSpec((tm, tn), lambda j, b, be, nu: (b, j)),
        scratch_shapes=[pltpu.VMEM((d, tn), BF16), pltpu.VMEM((d, tn), BF16)],
    )
    b3 = b_gu.reshape(n_e, 1, two_ff)
    return pl.pallas_call(
        _moe_gu_kernel,
        grid_spec=grid_spec,
        out_shape=jax.ShapeDtypeStruct((n_slots, d_ff), BF16),
        compiler_params=_params(("arbitrary", "arbitrary")),
        name="moe_gate_up",
    )(block_expert, nused, xs, w_gu, w_gu, b3, b3)


def _moe_down_kernel(be_ref, nused_ref, h_ref, w_ref, bias_ref, o_ref, w_bf):
    b = pl.program_id(1)

    @pl.when(_expert_changed(be_ref, b))
    def _():
        w_bf[...] = w_ref[...].astype(BF16)

    @pl.when(b < nused_ref[0])
    def _():
        o_ref[...] = jnp.dot(h_ref[...], w_bf[...], preferred_element_type=F32) + bias_ref[...]

    @pl.when(b >= nused_ref[0])
    def _():
        o_ref[...] = jnp.zeros(o_ref.shape, o_ref.dtype)


def moe_down(hdn, w_down, b_down, block_expert, nused):
    n_slots, d_ff = hdn.shape
    n_e, _, d = w_down.shape
    tm = MOE_TM
    tn = _tile(d, 1024)
    nb = n_slots // tm
    grid_spec = pltpu.PrefetchScalarGridSpec(
        num_scalar_prefetch=2,
        grid=(d // tn, nb),
        in_specs=[pl.BlockSpec((tm, d_ff), lambda j, b, be, nu: (b, 0)),
                  pl.BlockSpec((None, d_ff, tn), lambda j, b, be, nu: (be[b], 0, j)),
                  pl.BlockSpec((None, 1, tn), lambda j, b, be, nu: (be[b], 0, j))],
        out_specs=pl.BlockSpec((tm, tn), lambda j, b, be, nu: (b, j)),
        scratch_shapes=[pltpu.VMEM((d_ff, tn), BF16)],
    )
    return pl.pallas_call(
        _moe_down_kernel,
        grid_spec=grid_spec,
        out_shape=jax.ShapeDtypeStruct((n_slots, d), F32),
        compiler_params=_params(("arbitrary", "arbitrary")),
        name="moe_down",
    )(block_expert, nused, hdn, w_down, b_down.reshape(n_e, 1, d))


def _moe_combine_kernel(pos_ref, gates_ref, x_ref, y_ref, gain_ref, g_ref, o_ref, buf, sem, *, tm):
    def issue(r, carry):
        for k in range(TOP_K):
            pltpu.make_async_copy(y_ref.at[pl.ds(pos_ref[r * TOP_K + k], 1), :],
                                  buf.at[k, pl.ds(r, 1), :], sem).start()
        return carry
    lax.fori_loop(0, tm, issue, 0)
    for k in range(TOP_K):
        pltpu.make_async_copy(y_ref.at[pl.ds(0, tm), :], buf.at[k], sem).wait()
    gates = gates_ref[...]
    f = buf[0] * gates[:, 0:1]
    for k in range(1, TOP_K):
        f = f + buf[k] * gates[:, k:k + 1]
    y = f * lax.rsqrt(jnp.mean(f * f, axis=-1, keepdims=True) + RMS_EPS) * gain_ref[...]
    o_ref[...] = x_ref[...] + g_ref[...] * y


def moe_combine_resid(x, y, pos, gates, gain, gate_mod, n_prompt, rows_per_sample):
    n, d = x.shape
    tm = _tile(math.gcd(n_prompt, rows_per_sample), MOE_COMBINE_TM)
    seg = functools.partial(_seg_of_block, tm=tm, n_prompt=n_prompt, rows_per_sample=rows_per_sample)
    s = gate_mod.shape[0]
    nb = n // tm
    kern = functools.partial(_moe_combine_kernel, tm=tm)
    return pl.pallas_call(
        kern,
        grid=(nb,),
        in_specs=[pl.BlockSpec((None, None, tm * TOP_K), lambda i: (i, 0, 0), memory_space=pltpu.SMEM),
                  pl.BlockSpec((tm, TOP_K), lambda i: (i, 0)),
                  pl.BlockSpec((tm, d), lambda i: (i, 0)),
                  pl.BlockSpec(memory_space=pl.ANY),
                  pl.BlockSpec((1, d), lambda i: (0, 0)),
                  pl.BlockSpec((None, 1, d), lambda i: (seg(i), 0, 0))],
        out_specs=pl.BlockSpec((tm, d), lambda i: (i, 0)),
        out_shape=jax.ShapeDtypeStruct((n, d), F32),
        scratch_shapes=[pltpu.VMEM((TOP_K, tm, d), F32), pltpu.SemaphoreType.DMA(())],
        compiler_params=_params(("arbitrary",)),
        name="moe_combine",
    )(pos.reshape(nb, 1, tm * TOP_K), gates, x, y, gain.reshape(1, d), gate_mod.reshape(s, 1, d))


def moe_route(h, w_r, b_r):
    n, d = h.shape
    w_pad = jnp.pad(w_r, ((0, 0), (0, LANE - N_EXPERTS))).astype(BF16)
    logits = mm(h, w_pad, tm=512, tn=LANE)[:, :N_EXPERTS] + b_r
    top_logit, top_idx = lax.top_k(logits, TOP_K)
    gates = jax.nn.softmax(top_logit, axis=-1)
    a = n * TOP_K
    flat_e = top_idx.reshape(a)
    onehot = (flat_e[:, None] == jnp.arange(N_EXPERTS, dtype=flat_e.dtype)[None, :]).astype(jnp.int32)
    csum = jnp.cumsum(onehot, axis=0)
    rank = jnp.take_along_axis(csum, flat_e[:, None], axis=1)[:, 0] - 1
    counts = csum[-1]
    padded = (counts + MOE_TM - 1) // MOE_TM * MOE_TM
    pad_end = jnp.cumsum(padded)
    pad_start = pad_end - padded
    dest = (pad_start[flat_e] + rank).astype(jnp.int32)
    n_blocks = a // MOE_TM + N_EXPERTS
    n_slots = n_blocks * MOE_TM
    slot_tok = jnp.zeros((n_slots,), jnp.int32).at[dest].set(jnp.arange(a, dtype=jnp.int32) // TOP_K)
    block_expert = jnp.minimum(
        jnp.searchsorted(pad_end, jnp.arange(n_blocks, dtype=jnp.int32) * MOE_TM, side='right'),
        N_EXPERTS - 1).astype(jnp.int32)
    nused = (pad_end[-1] // MOE_TM).astype(jnp.int32).reshape(1)
    return slot_tok, block_expert, nused, dest.reshape(n, TOP_K), gates, n_slots


def moe_block(x, h, w_r, b_r, w_gu, b_gu, w_down, b_down, gain, gate_mod, n_prompt, rows_per_sample):
    slot_tok, block_expert, nused, pos, gates, n_slots = moe_route(h, w_r, b_r)
    xs = moe_gather(h, slot_tok, nused, n_slots)
    hdn = moe_gate_up(xs, w_gu, b_gu, block_expert, nused)
    y = moe_down(hdn, w_down, b_down, block_expert, nused)
    return moe_combine_resid(x, y, pos, gates, gain, gate_mod, n_prompt, rows_per_sample)


def _rms(x, gain):
    return x * lax.rsqrt(jnp.mean(x * x, axis=-1, keepdims=True) + RMS_EPS) * gain


def _rope_tables(n_tokens, rot_dim):
    rows = n_tokens // GRID_W
    row = jnp.repeat(jnp.arange(rows, dtype=F32), GRID_W)
    col = jnp.tile(jnp.arange(GRID_W, dtype=F32), rows)
    axis_dim = rot_dim // 2
    inv_freq = ROPE_THETA ** (-jnp.arange(0, axis_dim, 2, dtype=F32) / axis_dim)
    ang = jnp.concatenate([row[:, None] * inv_freq, col[:, None] * inv_freq], axis=-1)
    return jnp.cos(ang), jnp.sin(ang)


def _rope(x, cos, sin):
    shape = (cos.shape[0],) + (1,) * (x.ndim - 3) + (cos.shape[1],)
    c = cos.reshape(shape)
    s = sin.reshape(shape)
    x1, x2 = x[..., 0::2], x[..., 1::2]
    return jnp.stack([x1 * c - x2 * s, x1 * s + x2 * c], axis=-1).reshape(x.shape)


def _keys(cache, own_sample, own_prompt):
    bs = cache.shape[0]
    w = own_prompt.shape[-1]
    s = jnp.concatenate([cache.reshape(bs, -1, w), own_sample.reshape(bs, -1, w)], axis=1)
    return jnp.concatenate([s.reshape(-1, w), own_prompt], axis=0)


def _attend_groups(make_parts, *, dims, heads, dv, scale, out_dtype):
    bp, tp, bs, ts, past = dims
    q_parts, k_parts, v_part = make_parts
    o_p = attention(q_parts, k_parts, v_part, batch=bp, heads=heads, tq_total=tp, tk_total=tp,
                    q_row0=0, k_row0=bs * (past + ts), dv=dv, scale=scale, out_dtype=out_dtype)
    o_s = attention(q_parts, k_parts, v_part, batch=bs, heads=heads, tq_total=ts, tk_total=past + ts,
                    q_row0=bp * tp, k_row0=0, dv=dv, scale=scale, out_dtype=out_dtype)
    return jnp.concatenate([o_p, o_s], axis=0)


def even_mixer(h, dims, ropes, caches, w_in, q_norm, w_uq, kv_norm, w_ukv, lam_params, subln, w_out,
               lambda_init):
    bp, tp, bs, ts, past = dims
    n_p = bp * tp
    n = h.shape[0]
    c1 = MLA_Q_RANK + MLA_KV_RANK
    c2 = c1 + MLA_ROPE_DIM
    w_in = w_in.astype(BF16)
    a = mm(h, w_in[:, :c1])
    kr = mm(h, jnp.pad(w_in[:, c1:c2], ((0, 0), (0, LANE - MLA_ROPE_DIM))), tn=LANE)[:, :MLA_ROPE_DIM]
    df = mm(h, w_in[:, c2:])
    cq, ckv = a[:, :MLA_Q_RANK], a[:, MLA_Q_RANK:]
    w_uq3 = w_uq.reshape(MLA_Q_RANK, MLA_HEADS, MLA_NOPE_DIM + MLA_ROPE_DIM)
    w_uq_perm = jnp.concatenate([w_uq3[:, :, :MLA_NOPE_DIM].reshape(MLA_Q_RANK, -1),
                                 w_uq3[:, :, MLA_NOPE_DIM:].reshape(MLA_Q_RANK, -1)], axis=1).astype(BF16)
    q = mm(_rms(cq, q_norm).astype(BF16), w_uq_perm)
    n_nope = MLA_HEADS * MLA_NOPE_DIM
    q_nope = q[:, :n_nope].astype(BF16)
    q_rope = q[:, n_nope:].reshape(n, MLA_HEADS, MLA_ROPE_DIM)
    ckv = _rms(ckv, kv_norm)
    dq = df[:, :DIFF_QK_WIDTH].reshape(n, DIFF_HEADS, 2, DIFF_HEAD_DIM)
    dk = df[:, DIFF_QK_WIDTH:2 * DIFF_QK_WIDTH].reshape(n, DIFF_HEADS, 2, DIFF_HEAD_DIM)
    dv = df[:, 2 * DIFF_QK_WIDTH:]

    (cm, sm), (cd, sd) = ropes[0], ropes[1]

    def roped(x, cos, sin):
        xs = x[n_p:].reshape((bs, ts) + x.shape[1:])
        return jnp.concatenate([x[:n_p], _rope(xs, cos, sin).reshape((bs * ts,) + x.shape[1:])], axis=0)

    q_rope = roped(q_rope, cm, sm)
    kr_r = roped(kr, cm, sm)
    dq = roped(dq, cd, sd)
    dk_r = roped(dk, cd, sd)

    own = (ckv[:n_p].reshape(bp, tp, MLA_KV_RANK), kr[:n_p].reshape(bp, tp, MLA_ROPE_DIM),
           dk[:n_p].reshape(bp, tp, DIFF_HEADS, 2 * DIFF_HEAD_DIM),
           dv[:n_p].reshape(bp, tp, DIFF_HEADS, DIFF_V_DIM))

    c_ckv, c_kr, c_dk, c_dv = caches
    ckv_keys = _keys(c_ckv, ckv[n_p:], ckv[:n_p]).astype(BF16)
    kv = mm(ckv_keys, w_ukv.astype(BF16), out_dtype=BF16)
    kr_keys = jnp.pad(_keys(c_kr, kr_r[n_p:], kr_r[:n_p]), ((0, 0), (0, LANE - MLA_ROPE_DIM))).astype(BF16)
    q_rope_pad = jnp.pad(q_rope, ((0, 0), (0, 0), (0, LANE - MLA_ROPE_DIM))).reshape(n, -1).astype(BF16)
    a_out = _attend_groups(
        ([(q_nope, lambda hh: hh), (q_rope_pad, lambda hh: hh)],
         [(kv, lambda hh: 2 * hh), (kr_keys, lambda hh: 0)],
         (kv, lambda hh: 2 * hh + 1)),
        dims=dims, heads=MLA_HEADS, dv=MLA_V_DIM,
        scale=(MLA_NOPE_DIM + MLA_ROPE_DIM) ** -0.5, out_dtype=BF16)

    dk_keys = _keys(c_dk, dk_r[n_p:].reshape(bs * ts, -1), dk_r[:n_p].reshape(n_p, -1)).astype(BF16)
    dv_keys = _keys(c_dv, dv[n_p:], dv[:n_p]).astype(BF16)
    o2 = _attend_groups(
        ([(dq.reshape(n, -1).astype(BF16), lambda hh: hh)],
         [(dk_keys, lambda hh: hh)],
         (dv_keys, lambda hh: hh // 2)),
        dims=dims, heads=2 * DIFF_HEADS, dv=DIFF_V_DIM,
        scale=DIFF_HEAD_DIM ** -0.5, out_dtype=F32).reshape(n, DIFF_HEADS, 2, DIFF_V_DIM)
    lq1, lk1, lq2, lk2 = lam_params.astype(F32)
    lam = jnp.exp(jnp.sum(lq1 * lk1)) - jnp.exp(jnp.sum(lq2 * lk2)) + lambda_init
    b_out = o2[:, :, 0] - lam * o2[:, :, 1]
    b_out = _rms(b_out, subln) * (1.0 - lambda_init)
    merged = jnp.concatenate([a_out, b_out.reshape(n, -1).astype(BF16)], axis=-1)
    return mm(merged, w_out.astype(BF16)), own


def odd_mixer(h, dims, ropes, caches, w_in, q_norm, k_norm, w_out):
    bp, tp, bs, ts, past = dims
    n_p = bp * tp
    n = h.shape[0]
    qkv = mm(h, w_in.astype(BF16))
    nq = GQA_Q_HEADS * GQA_HEAD_DIM
    nk = GQA_KV_HEADS * GQA_HEAD_DIM
    q = _rms(qkv[:, :nq].reshape(n, GQA_Q_HEADS, GQA_HEAD_DIM), q_norm)
    k = _rms(qkv[:, nq:nq + nk].reshape(n, GQA_KV_HEADS, GQA_HEAD_DIM), k_norm)
    v = qkv[:, nq + nk:]
    cg, sg = ropes[2]

    def roped(x):
        xs = x[n_p:].reshape((bs, ts) + x.shape[1:])
        return jnp.concatenate([x[:n_p], _rope(xs, cg, sg).reshape((bs * ts,) + x.shape[1:])], axis=0)

    q_r = roped(q)
    k_r = roped(k)
    own = (k[:n_p].reshape(bp, tp, GQA_KV_HEADS, GQA_HEAD_DIM),
           v[:n_p].reshape(bp, tp, GQA_KV_HEADS, GQA_HEAD_DIM))
    c_k, c_v = caches
    k_keys = _keys(c_k, k_r[n_p:].reshape(bs * ts, -1), k_r[:n_p].reshape(n_p, -1)).astype(BF16)
    v_keys = _keys(c_v, v[n_p:], v[:n_p]).astype(BF16)
    rep = GQA_Q_HEADS // GQA_KV_HEADS
    o = _attend_groups(
        ([(q_r.reshape(n, -1).astype(BF16), lambda hh: hh)],
         [(k_keys, lambda hh: hh // rep)],
         (v_keys, lambda hh: hh // rep)),
        dims=dims, heads=GQA_Q_HEADS, dv=GQA_HEAD_DIM, scale=GQA_HEAD_DIM ** -0.5, out_dtype=BF16)
    return mm(o, w_out.astype(BF16)), own


def kernel(x_prompt, x_sample, cache_mla_ckv, cache_mla_krope, cache_diff_k, cache_diff_v, cache_gqa_k, cache_gqa_v, c, c_ctx, mod_w, mod_b, norm_gains, even_w_in, mla_q_norm, mla_w_uq, mla_kv_norm, mla_w_ukv, diff_lambda, diff_subln, even_w_out, odd_w_in, gqa_q_norm, gqa_k_norm, odd_w_out, router_w, router_b, moe_w_gu, moe_b_gu, moe_w_down, moe_b_down):
    bp, tp, d = x_prompt.shape
    bs, ts, _ = x_sample.shape
    past = cache_mla_ckv.shape[2]
    depth = mod_w.shape[0]
    n_p = bp * tp
    dims = (bp, tp, bs, ts, past)
    x = jnp.concatenate([x_prompt.reshape(n_p, d), x_sample.reshape(bs * ts, d)], axis=0)
    cond = jnp.concatenate([c_ctx[None, :], c], axis=0)
    n_seg = cond.shape[0]
    cond_act = jnp.pad(jax.nn.silu(cond), ((0, (-n_seg) % 16), (0, 0))).astype(BF16)
    ropes = (_rope_tables(ts, MLA_ROPE_DIM), _rope_tables(ts, DIFF_HEAD_DIM), _rope_tables(ts, GQA_HEAD_DIM))

    even_states, odd_states = [], []
    for layer in range(depth):
        mod = mm(cond_act, mod_w[layer])[:n_seg] + mod_b[layer]
        sh_m, sc_m, g_m, sh_f, sc_f, g_f = jnp.split(mod, 6, axis=-1)
        gains = norm_gains[layer]
        h = norm_mod(x, gains[0], sc_m, sh_m, n_p, ts, BF16)
        i = layer // 2
        if layer % 2 == 0:
            caches = (cache_mla_ckv[:, i], cache_mla_krope[:, i], cache_diff_k[:, i], cache_diff_v[:, i])
            lambda_init = 0.8 - 0.6 * math.exp(-0.3 * layer)
            m, own = even_mixer(h, dims, ropes, caches, even_w_in[i], mla_q_norm[i], mla_w_uq[i],
                                mla_kv_norm[i], mla_w_ukv[i], diff_lambda[i], diff_subln[i], even_w_out[i],
                                lambda_init)
            even_states.append(own)
        else:
            caches = (cache_gqa_k[:, i], cache_gqa_v[:, i])
            m, own = odd_mixer(h, dims, ropes, caches, odd_w_in[i], gqa_q_norm[i], gqa_k_norm[i], odd_w_out[i])
            odd_states.append(own)
        x = resid(x, m, gains[1], g_m, n_p, ts)
        h = norm_mod(x, gains[2], sc_f, sh_f, n_p, ts, F32)
        x = moe_block(x, h, router_w[layer], router_b[layer], moe_w_gu[layer], moe_b_gu[layer],
                      moe_w_down[layer], moe_b_down[layer], gains[3], g_f, n_p, ts)

    y_prompt = x[:n_p].reshape(bp, tp, d)
    y_sample = x[n_p:].reshape(bs, ts, d)
    return (y_prompt, y_sample,
            jnp.stack([s[0] for s in even_states], axis=1),
            jnp.stack([s[1] for s in even_states], axis=1),
            jnp.stack([s[2] for s in even_states], axis=1),
            jnp.stack([s[3] for s in even_states], axis=1),
            jnp.stack([s[0] for s in odd_states], axis=1),
            jnp.stack([s[1] for s in odd_states], axis=1))
```

```python
import functools
import math

import jax
import jax.numpy as jnp
from jax import lax
from jax.experimental import pallas as pl
from jax.experimental.pallas import tpu as pltpu

F32 = jnp.float32
BF16 = jnp.bfloat16

GRID_W = 64
ROPE_THETA = 10000.0
RMS_EPS = 1e-6
MLA_HEADS = 16
MLA_Q_RANK = 1024
MLA_KV_RANK = 512
MLA_NOPE_DIM = 128
MLA_ROPE_DIM = 64
MLA_V_DIM = 128
DIFF_HEADS = 8
DIFF_HEAD_DIM = 128
DIFF_V_DIM = 2 * DIFF_HEAD_DIM
DIFF_QK_WIDTH = DIFF_HEADS * 2 * DIFF_HEAD_DIM
GQA_Q_HEADS = 32
GQA_KV_HEADS = 8
GQA_HEAD_DIM = 128
N_EXPERTS = 32
TOP_K = 4
SWIGLU_LIMIT = 7.0
SWIGLU_ALPHA = 1.702

LANE = 128
ATTN_TQ = 1024
ATTN_SUB = 256
VMEM_LIMIT = 56 * 1024 * 1024


def _tile(dim, pref):
    t = min(pref, dim)
    while dim % t:
        t //= 2
    return t


def _params(sem):
    return pltpu.CompilerParams(dimension_semantics=sem, vmem_limit_bytes=VMEM_LIMIT)


def _mm_kernel(x_ref, w_ref, o_ref):
    o_ref[...] = jnp.dot(x_ref[...].astype(BF16), w_ref[...].astype(BF16),
                         preferred_element_type=F32).astype(o_ref.dtype)


def mm(x, w, out_dtype=F32, tm=1024, tn=512, w_index=None):
    m, k = x.shape
    n = w.shape[-1]
    tm = _tile(m, tm)
    tn = _tile(n, tn)
    if w_index is None:
        w_spec = pl.BlockSpec((k, tn), lambda i, j: (0, j))
    else:
        w_spec = pl.BlockSpec((None, k, tn), lambda i, j: (w_index, 0, j))
    return pl.pallas_call(
        _mm_kernel,
        grid=(m // tm, n // tn),
        in_specs=[pl.BlockSpec((tm, k), lambda i, j: (i, 0)), w_spec],
        out_specs=pl.BlockSpec((tm, tn), lambda i, j: (i, j)),
        out_shape=jax.ShapeDtypeStruct((m, n), out_dtype),
        compiler_params=_params(("parallel", "parallel")),
        name="mm",
    )(x, w)


def _seg_of_block(i, tm, n_prompt, rows_per_sample):
    pb = n_prompt // tm
    return jnp.where(i < pb, 0, 1 + (i - pb) // (rows_per_sample // tm))


def _norm_mod_kernel(x_ref, gain_ref, sc_ref, sh_ref, o_ref):
    x = x_ref[...]
    y = x * lax.rsqrt(jnp.mean(x * x, axis=-1, keepdims=True) + RMS_EPS) * gain_ref[...]
    o_ref[...] = (y * (1.0 + sc_ref[...]) + sh_ref[...]).astype(o_ref.dtype)


def norm_mod(x, gain, scale, shift, n_prompt, rows_per_sample, out_dtype):
    n, d = x.shape
    tm = _tile(math.gcd(n_prompt, rows_per_sample), 256)
    seg = functools.partial(_seg_of_block, tm=tm, n_prompt=n_prompt, rows_per_sample=rows_per_sample)
    s = scale.shape[0]
    return pl.pallas_call(
        _norm_mod_kernel,
        grid=(n // tm,),
        in_specs=[pl.BlockSpec((tm, d), lambda i: (i, 0)),
                  pl.BlockSpec((1, d), lambda i: (0, 0)),
                  pl.BlockSpec((None, 1, d), lambda i: (seg(i), 0, 0)),
                  pl.BlockSpec((None, 1, d), lambda i: (seg(i), 0, 0))],
        out_specs=pl.BlockSpec((tm, d), lambda i: (i, 0)),
        out_shape=jax.ShapeDtypeStruct((n, d), out_dtype),
        compiler_params=_params(("parallel",)),
        name="norm_mod",
    )(x, gain.reshape(1, d), scale.reshape(s, 1, d), shift.reshape(s, 1, d))


def _resid_kernel(x_ref, m_ref, gain_ref, g_ref, o_ref):
    m = m_ref[...]
    y = m * lax.rsqrt(jnp.mean(m * m, axis=-1, keepdims=True) + RMS_EPS) * gain_ref[...]
    o_ref[...] = x_ref[...] + g_ref[...] * y


def resid(x, m, gain, gate, n_prompt, rows_per_sample):
    n, d = x.shape
    tm = _tile(math.gcd(n_prompt, rows_per_sample), 256)
    seg = functools.partial(_seg_of_block, tm=tm, n_prompt=n_prompt, rows_per_sample=rows_per_sample)
    s = gate.shape[0]
    return pl.pallas_call(
        _resid_kernel,
        grid=(n // tm,),
        in_specs=[pl.BlockSpec((tm, d), lambda i: (i, 0)),
                  pl.BlockSpec((tm, d), lambda i: (i, 0)),
                  pl.BlockSpec((1, d), lambda i: (0, 0)),
                  pl.BlockSpec((None, 1, d), lambda i: (seg(i), 0, 0))],
        out_specs=pl.BlockSpec((tm, d), lambda i: (i, 0)),
        out_shape=jax.ShapeDtypeStruct((n, d), F32),
        compiler_params=_params(("parallel",)),
        name="resid",
    )(x, m, gain.reshape(1, d), gate.reshape(s, 1, d))


def _attn_kernel(*refs, n_q, n_k, exp2_scale):
    q_refs = refs[:n_q]
    k_refs = refs[n_q:n_q + n_k]
    v_ref = refs[n_q + n_k]
    o_ref = refs[n_q + n_k + 1]

    if n_k > 1:
        kcat = refs[n_q + n_k + 2]

        @pl.when(pl.program_id(2) == 0)
        def _():
            for idx, r in enumerate(k_refs):
                kcat[:, idx * LANE:(idx + 1) * LANE] = r[...]
        k = kcat[...]
    else:
        k = k_refs[0][...]
    v = v_ref[...]
    tq = o_ref.shape[0]
    sub = min(tq, ATTN_SUB)
    for r0 in range(0, tq, sub):
        qs = [r[r0:r0 + sub, :] for r in q_refs]
        q = jnp.concatenate(qs, axis=1) if n_q > 1 else qs[0]
        s = lax.dot_general(q, k, (((1,), (1,)), ((), ())), preferred_element_type=F32)
        m = jnp.max(s, axis=1, keepdims=True)
        p = jnp.exp2((s - m) * exp2_scale)
        l = jnp.sum(p, axis=1, keepdims=True)
        o = jnp.dot(p.astype(BF16), v, preferred_element_type=F32)
        o_ref[r0:r0 + sub, :] = (o / l).astype(o_ref.dtype)


def attention(q_parts, k_parts, v_part, *, batch, heads, tq_total, tk_total, q_row0, k_row0,
              dv, scale, out_dtype):
    tq = _tile(tq_total, ATTN_TQ)
    nq = tq_total // tq
    assert q_row0 % tq == 0 and k_row0 % tk_total == 0
    qb0, kb0 = q_row0 // tq, k_row0 // tk_total

    def q_spec(fn):
        return pl.BlockSpec((tq, LANE), lambda b, h, i: (qb0 + b * nq + i, fn(h)))

    def k_spec(fn, width):
        return pl.BlockSpec((tk_total, width), lambda b, h, i: (kb0 + b, fn(h)))

    in_specs = ([q_spec(fn) for _, fn in q_parts] + [k_spec(fn, LANE) for _, fn in k_parts]
                + [k_spec(v_part[1], dv)])
    n_k = len(k_parts)
    kern = functools.partial(_attn_kernel, n_q=len(q_parts), n_k=n_k, exp2_scale=scale * math.log2(math.e))
    scratch = [pltpu.VMEM((tk_total, n_k * LANE), BF16)] if n_k > 1 else []
    return pl.pallas_call(
        kern,
        grid=(batch, heads, nq),
        in_specs=in_specs,
        out_specs=pl.BlockSpec((tq, dv), lambda b, h, i: (b * nq + i, h)),
        out_shape=jax.ShapeDtypeStruct((batch * tq_total, heads * dv), out_dtype),
        scratch_shapes=scratch,
        compiler_params=_params(("parallel", "parallel", "arbitrary")),
        name="attention",
    )(*[a for a, _ in q_parts], *[a for a, _ in k_parts], v_part[0])


MOE_TM = 512
MOE_GATHER_TM = 256
MOE_COMBINE_TM = 128


def _moe_gather_kernel(tok_ref, nused_ref, h_ref, o_ref, buf, sem, *, tm, blocks_per_moe_block):
    b = pl.program_id(0)
    used = b < nused_ref[0] * blocks_per_moe_block

    @pl.when(used)
    def _():
        def issue(r, carry):
            pltpu.make_async_copy(h_ref.at[pl.ds(tok_ref[r], 1), :], buf.at[pl.ds(r, 1), :], sem).start()
            return carry
        lax.fori_loop(0, tm, issue, 0)
        pltpu.make_async_copy(h_ref.at[pl.ds(0, tm), :], buf, sem).wait()
        o_ref[...] = buf[...].astype(o_ref.dtype)

    @pl.when(jnp.logical_not(used))
    def _():
        o_ref[...] = jnp.zeros(o_ref.shape, o_ref.dtype)


def moe_gather(h, slot_tok, nused, n_slots):
    n, d = h.shape
    tm = MOE_GATHER_TM
    nb = n_slots // tm
    kern = functools.partial(_moe_gather_kernel, tm=tm, blocks_per_moe_block=MOE_TM // tm)
    return pl.pallas_call(
        kern,
        grid=(nb,),
        in_specs=[pl.BlockSpec((None, None, tm), lambda b: (b, 0, 0), memory_space=pltpu.SMEM),
                  pl.BlockSpec(memory_space=pltpu.SMEM),
                  pl.BlockSpec(memory_space=pl.ANY)],
        out_specs=pl.BlockSpec((tm, d), lambda b: (b, 0)),
        out_shape=jax.ShapeDtypeStruct((n_slots, d), BF16),
        scratch_shapes=[pltpu.VMEM((tm, d), F32), pltpu.SemaphoreType.DMA(())],
        compiler_params=_params(("arbitrary",)),
        name="moe_gather",
    )(slot_tok.reshape(nb, 1, tm), nused, h)


def _expert_changed(be_ref, b):
    return jnp.logical_or(b == 0, be_ref[b] != be_ref[jnp.maximum(b - 1, 0)])


def _moe_gu_kernel(be_ref, nused_ref, x_ref, wg_ref, wu_ref, bg_ref, bu_ref, o_ref, wg_bf, wu_bf):
    b = pl.program_id(1)

    @pl.when(_expert_changed(be_ref, b))
    def _():
        wg_bf[...] = wg_ref[...].astype(BF16)
        wu_bf[...] = wu_ref[...].astype(BF16)

    @pl.when(b < nused_ref[0])
    def _():
        x = x_ref[...]
        g = jnp.dot(x, wg_bf[...], preferred_element_type=F32) + bg_ref[...]
        u = jnp.dot(x, wu_bf[...], preferred_element_type=F32) + bu_ref[...]
        gate_h = jnp.minimum(g, SWIGLU_LIMIT)
        up_h = jnp.clip(u, -SWIGLU_LIMIT, SWIGLU_LIMIT)
        o_ref[...] = ((up_h + 1.0) * gate_h * jax.nn.sigmoid(SWIGLU_ALPHA * gate_h)).astype(o_ref.dtype)

    @pl.when(b >= nused_ref[0])
    def _():
        o_ref[...] = jnp.zeros(o_ref.shape, o_ref.dtype)


def moe_gate_up(xs, w_gu, b_gu, layer, block_expert, nused):
    n_slots, d = xs.shape
    depth, n_e, _, two_ff = w_gu.shape
    d_ff = two_ff // 2
    tm = MOE_TM
    tn = _tile(d_ff, 256)
    nj = d_ff // tn
    nb = n_slots // tm
    grid_spec = pltpu.PrefetchScalarGridSpec(
        num_scalar_prefetch=2,
        grid=(nj, nb),
        in_specs=[pl.BlockSpec((tm, d), lambda j, b, be, nu: (b, 0)),
                  pl.BlockSpec((None, None, d, tn), lambda j, b, be, nu: (layer, be[b], 0, j)),
                  pl.BlockSpec((None, None, d, tn), lambda j, b, be, nu: (layer, be[b], 0, nj + j)),
                  pl.BlockSpec((None, None, 1, tn), lambda j, b, be, nu: (layer, be[b], 0, j)),
                  pl.BlockSpec((None, None, 1, tn), lambda j, b, be, nu: (layer, be[b], 0, nj + j))],
        out_specs=pl.BlockSpec((tm, tn), lambda j, b, be, nu: (b, j)),
        scratch_shapes=[pltpu.VMEM((d, tn), BF16), pltpu.VMEM((d, tn), BF16)],
    )
    b3 = b_gu.reshape(depth, n_e, 1, two_ff)
    return pl.pallas_call(
        _moe_gu_kernel,
        grid_spec=grid_spec,
        out_shape=jax.ShapeDtypeStruct((n_slots, d_ff), BF16),
        compiler_params=_params(("arbitrary", "arbitrary")),
        name="moe_gate_up",
    )(block_expert, nused, xs, w_gu, w_gu, b3, b3)


def _moe_down_kernel(be_ref, nused_ref, h_ref, w_ref, bias_ref, o_ref, w_bf):
    b = pl.program_id(1)

    @pl.when(_expert_changed(be_ref, b))
    def _():
        w_bf[...] = w_ref[...].astype(BF16)

    @pl.when(b < nused_ref[0])
    def _():
        o_ref[...] = jnp.dot(h_ref[...], w_bf[...], preferred_element_type=F32) + bias_ref[...]

    @pl.when(b >= nused_ref[0])
    def _():
        o_ref[...] = jnp.zeros(o_ref.shape, o_ref.dtype)


def moe_down(hdn, w_down, b_down, layer, block_expert, nused):
    n_slots, d_ff = hdn.shape
    depth, n_e, _, d = w_down.shape
    tm = MOE_TM
    tn = _tile(d, 1024)
    nb = n_slots // tm
    grid_spec = pltpu.PrefetchScalarGridSpec(
        num_scalar_prefetch=2,
        grid=(d // tn, nb),
        in_specs=[pl.BlockSpec((tm, d_ff), lambda j, b, be, nu: (b, 0)),
                  pl.BlockSpec((None, None, d_ff, tn), lambda j, b, be, nu: (layer, be[b], 0, j)),
                  pl.BlockSpec((None, None, 1, tn), lambda j, b, be, nu: (layer, be[b], 0, j))],
        out_specs=pl.BlockSpec((tm, tn), lambda j, b, be, nu: (b, j)),
        scratch_shapes=[pltpu.VMEM((d_ff, tn), BF16)],
    )
    return pl.pallas_call(
        _moe_down_kernel,
        grid_spec=grid_spec,
        out_shape=jax.ShapeDtypeStruct((n_slots, d), F32),
        compiler_params=_params(("arbitrary", "arbitrary")),
        name="moe_down",
    )(block_expert, nused, hdn, w_down, b_down.reshape(depth, n_e, 1, d))


def _moe_combine_kernel(pos_ref, gates_ref, x_ref, y_ref, gain_ref, g_ref, o_ref, buf, sem, *, tm):
    def issue(r, carry):
        for k in range(TOP_K):
            pltpu.make_async_copy(y_ref.at[pl.ds(pos_ref[r * TOP_K + k], 1), :],
                                  buf.at[k, pl.ds(r, 1), :], sem).start()
        return carry
    lax.fori_loop(0, tm, issue, 0)
    for k in range(TOP_K):
        pltpu.make_async_copy(y_ref.at[pl.ds(0, tm), :], buf.at[k], sem).wait()
    gates = gates_ref[...]
    f = buf[0] * gates[:, 0:1]
    for k in range(1, TOP_K):
        f = f + buf[k] * gates[:, k:k + 1]
    y = f * lax.rsqrt(jnp.mean(f * f, axis=-1, keepdims=True) + RMS_EPS) * gain_ref[...]
    o_ref[...] = x_ref[...] + g_ref[...] * y


def moe_combine_resid(x, y, pos, gates, gain, gate_mod, n_prompt, rows_per_sample):
    n, d = x.shape
    tm = _tile(math.gcd(n_prompt, rows_per_sample), MOE_COMBINE_TM)
    seg = functools.partial(_seg_of_block, tm=tm, n_prompt=n_prompt, rows_per_sample=rows_per_sample)
    s = gate_mod.shape[0]
    nb = n // tm
    kern = functools.partial(_moe_combine_kernel, tm=tm)
    return pl.pallas_call(
        kern,
        grid=(nb,),
        in_specs=[pl.BlockSpec((None, None, tm * TOP_K), lambda i: (i, 0, 0), memory_space=pltpu.SMEM),
                  pl.BlockSpec((tm, TOP_K), lambda i: (i, 0)),
                  pl.BlockSpec((tm, d), lambda i: (i, 0)),
                  pl.BlockSpec(memory_space=pl.ANY),
                  pl.BlockSpec((1, d), lambda i: (0, 0)),
                  pl.BlockSpec((None, 1, d), lambda i: (seg(i), 0, 0))],
        out_specs=pl.BlockSpec((tm, d), lambda i: (i, 0)),
        out_shape=jax.ShapeDtypeStruct((n, d), F32),
        scratch_shapes=[pltpu.VMEM((TOP_K, tm, d), F32), pltpu.SemaphoreType.DMA(())],
        compiler_params=_params(("arbitrary",)),
        name="moe_combine",
    )(pos.reshape(nb, 1, tm * TOP_K), gates, x, y, gain.reshape(1, d), gate_mod.reshape(s, 1, d))


def moe_route(h, w_r, b_r):
    n, d = h.shape
    w_pad = jnp.pad(w_r, ((0, 0), (0, LANE - N_EXPERTS))).astype(BF16)
    logits = mm(h, w_pad, tm=512, tn=LANE)[:, :N_EXPERTS] + b_r
    top_logit, top_idx = lax.top_k(logits, TOP_K)
    gates = jax.nn.softmax(top_logit, axis=-1)
    a = n * TOP_K
    flat_e = top_idx.reshape(a)
    onehot = (flat_e[:, None] == jnp.arange(N_EXPERTS, dtype=flat_e.dtype)[None, :]).astype(jnp.int32)
    csum = jnp.cumsum(onehot, axis=0)
    rank = jnp.take_along_axis(csum, flat_e[:, None], axis=1)[:, 0] - 1
    counts = csum[-1]
    padded = (counts + MOE_TM - 1) // MOE_TM * MOE_TM
    pad_end = jnp.cumsum(padded)
    pad_start = pad_end - padded
    dest = (pad_start[flat_e] + rank).astype(jnp.int32)
    n_blocks = a // MOE_TM + N_EXPERTS
    n_slots = n_blocks * MOE_TM
    slot_tok = jnp.zeros((n_slots,), jnp.int32).at[dest].set(jnp.arange(a, dtype=jnp.int32) // TOP_K)
    block_expert = jnp.minimum(
        jnp.searchsorted(pad_end, jnp.arange(n_blocks, dtype=jnp.int32) * MOE_TM, side='right'),
        N_EXPERTS - 1).astype(jnp.int32)
    nused = (pad_end[-1] // MOE_TM).astype(jnp.int32).reshape(1)
    return slot_tok, block_expert, nused, dest.reshape(n, TOP_K), gates, n_slots


def moe_block(x, h, w_r, b_r, w_gu, b_gu, w_down, b_down, layer, gain, gate_mod, n_prompt, rows_per_sample):
    slot_tok, block_expert, nused, pos, gates, n_slots = moe_route(h, w_r, b_r)
    xs = moe_gather(h, slot_tok, nused, n_slots)
    hdn = moe_gate_up(xs, w_gu, b_gu, layer, block_expert, nused)
    y = moe_down(hdn, w_down, b_down, layer, block_expert, nused)
    return moe_combine_resid(x, y, pos, gates, gain, gate_mod, n_prompt, rows_per_sample)


def _rms(x, gain):
    return x * lax.rsqrt(jnp.mean(x * x, axis=-1, keepdims=True) + RMS_EPS) * gain


def _rope_tables(n_tokens, rot_dim):
    rows = n_tokens // GRID_W
    row = jnp.repeat(jnp.arange(rows, dtype=F32), GRID_W)
    col = jnp.tile(jnp.arange(GRID_W, dtype=F32), rows)
    axis_dim = rot_dim // 2
    inv_freq = ROPE_THETA ** (-jnp.arange(0, axis_dim, 2, dtype=F32) / axis_dim)
    ang = jnp.concatenate([row[:, None] * inv_freq, col[:, None] * inv_freq], axis=-1)
    return jnp.cos(ang), jnp.sin(ang)


def _rope(x, cos, sin):
    shape = (cos.shape[0],) + (1,) * (x.ndim - 3) + (cos.shape[1],)
    c = cos.reshape(shape)
    s = sin.reshape(shape)
    x1, x2 = x[..., 0::2], x[..., 1::2]
    return jnp.stack([x1 * c - x2 * s, x1 * s + x2 * c], axis=-1).reshape(x.shape)


def _keys(cache, own_sample, own_prompt):
    bs = cache.shape[0]
    w = own_prompt.shape[-1]
    s = jnp.concatenate([cache.reshape(bs, -1, w), own_sample.reshape(bs, -1, w)], axis=1)
    return jnp.concatenate([s.reshape(-1, w), own_prompt], axis=0)


def _attend_groups(make_parts, *, dims, heads, dv, scale, out_dtype):
    bp, tp, bs, ts, past = dims
    q_parts, k_parts, v_part = make_parts
    o_p = attention(q_parts, k_parts, v_part, batch=bp, heads=heads, tq_total=tp, tk_total=tp,
                    q_row0=0, k_row0=bs * (past + ts), dv=dv, scale=scale, out_dtype=out_dtype)
    o_s = attention(q_parts, k_parts, v_part, batch=bs, heads=heads, tq_total=ts, tk_total=past + ts,
                    q_row0=bp * tp, k_row0=0, dv=dv, scale=scale, out_dtype=out_dtype)
    return jnp.concatenate([o_p, o_s], axis=0)


def even_mixer(h, dims, ropes, caches, w_in, q_norm, w_uq, kv_norm, w_ukv, lam_params, subln, w_out,
               lambda_init):
    bp, tp, bs, ts, past = dims
    n_p = bp * tp
    n = h.shape[0]
    c1 = MLA_Q_RANK + MLA_KV_RANK
    c2 = c1 + MLA_ROPE_DIM
    w_in = w_in.astype(BF16)
    a = mm(h, w_in[:, :c1])
    kr = mm(h, jnp.pad(w_in[:, c1:c2], ((0, 0), (0, LANE - MLA_ROPE_DIM))), tn=LANE)[:, :MLA_ROPE_DIM]
    df = mm(h, w_in[:, c2:])
    cq, ckv = a[:, :MLA_Q_RANK], a[:, MLA_Q_RANK:]
    w_uq3 = w_uq.reshape(MLA_Q_RANK, MLA_HEADS, MLA_NOPE_DIM + MLA_ROPE_DIM)
    w_uq_perm = jnp.concatenate([w_uq3[:, :, :MLA_NOPE_DIM].reshape(MLA_Q_RANK, -1),
                                 w_uq3[:, :, MLA_NOPE_DIM:].reshape(MLA_Q_RANK, -1)], axis=1).astype(BF16)
    q = mm(_rms(cq, q_norm).astype(BF16), w_uq_perm)
    n_nope = MLA_HEADS * MLA_NOPE_DIM
    q_nope = q[:, :n_nope].astype(BF16)
    q_rope = q[:, n_nope:].reshape(n, MLA_HEADS, MLA_ROPE_DIM)
    ckv = _rms(ckv, kv_norm)
    dq = df[:, :DIFF_QK_WIDTH].reshape(n, DIFF_HEADS, 2, DIFF_HEAD_DIM)
    dk = df[:, DIFF_QK_WIDTH:2 * DIFF_QK_WIDTH].reshape(n, DIFF_HEADS, 2, DIFF_HEAD_DIM)
    dv = df[:, 2 * DIFF_QK_WIDTH:]

    (cm, sm), (cd, sd) = ropes[0], ropes[1]

    def roped(x, cos, sin):
        xs = x[n_p:].reshape((bs, ts) + x.shape[1:])
        return jnp.concatenate([x[:n_p], _rope(xs, cos, sin).reshape((bs * ts,) + x.shape[1:])], axis=0)

    q_rope = roped(q_rope, cm, sm)
    kr_r = roped(kr, cm, sm)
    dq = roped(dq, cd, sd)
    dk_r = roped(dk, cd, sd)

    own = (ckv[:n_p].reshape(bp, tp, MLA_KV_RANK), kr[:n_p].reshape(bp, tp, MLA_ROPE_DIM),
           dk[:n_p].reshape(bp, tp, DIFF_HEADS, 2 * DIFF_HEAD_DIM),
           dv[:n_p].reshape(bp, tp, DIFF_HEADS, DIFF_V_DIM))

    c_ckv, c_kr, c_dk, c_dv = caches
    ckv_keys = _keys(c_ckv, ckv[n_p:], ckv[:n_p]).astype(BF16)
    kv = mm(ckv_keys, w_ukv.astype(BF16), out_dtype=BF16)
    kr_keys = jnp.pad(_keys(c_kr, kr_r[n_p:], kr_r[:n_p]), ((0, 0), (0, LANE - MLA_ROPE_DIM))).astype(BF16)
    q_rope_pad = jnp.pad(q_rope, ((0, 0), (0, 0), (0, LANE - MLA_ROPE_DIM))).reshape(n, -1).astype(BF16)
    a_out = _attend_groups(
        ([(q_nope, lambda hh: hh), (q_rope_pad, lambda hh: hh)],
         [(kv, lambda hh: 2 * hh), (kr_keys, lambda hh: 0)],
         (kv, lambda hh: 2 * hh + 1)),
        dims=dims, heads=MLA_HEADS, dv=MLA_V_DIM,
        scale=(MLA_NOPE_DIM + MLA_ROPE_DIM) ** -0.5, out_dtype=BF16)

    dk_keys = _keys(c_dk, dk_r[n_p:].reshape(bs * ts, -1), dk_r[:n_p].reshape(n_p, -1)).astype(BF16)
    dv_keys = _keys(c_dv, dv[n_p:], dv[:n_p]).astype(BF16)
    o2 = _attend_groups(
        ([(dq.reshape(n, -1).astype(BF16), lambda hh: hh)],
         [(dk_keys, lambda hh: hh)],
         (dv_keys, lambda hh: hh // 2)),
        dims=dims, heads=2 * DIFF_HEADS, dv=DIFF_V_DIM,
        scale=DIFF_HEAD_DIM ** -0.5, out_dtype=F32).reshape(n, DIFF_HEADS, 2, DIFF_V_DIM)
    lq1, lk1, lq2, lk2 = lam_params.astype(F32)
    lam = jnp.exp(jnp.sum(lq1 * lk1)) - jnp.exp(jnp.sum(lq2 * lk2)) + lambda_init
    b_out = o2[:, :, 0] - lam * o2[:, :, 1]
    b_out = _rms(b_out, subln) * (1.0 - lambda_init)
    merged = jnp.concatenate([a_out, b_out.reshape(n, -1).astype(BF16)], axis=-1)
    return mm(merged, w_out.astype(BF16)), own


def odd_mixer(h, dims, ropes, caches, w_in, q_norm, k_norm, w_out):
    bp, tp, bs, ts, past = dims
    n_p = bp * tp
    n = h.shape[0]
    qkv = mm(h, w_in.astype(BF16))
    nq = GQA_Q_HEADS * GQA_HEAD_DIM
    nk = GQA_KV_HEADS * GQA_HEAD_DIM
    q = _rms(qkv[:, :nq].reshape(n, GQA_Q_HEADS, GQA_HEAD_DIM), q_norm)
    k = _rms(qkv[:, nq:nq + nk].reshape(n, GQA_KV_HEADS, GQA_HEAD_DIM), k_norm)
    v = qkv[:, nq + nk:]
    cg, sg = ropes[2]

    def roped(x):
        xs = x[n_p:].reshape((bs, ts) + x.shape[1:])
        return jnp.concatenate([x[:n_p], _rope(xs, cg, sg).reshape((bs * ts,) + x.shape[1:])], axis=0)

    q_r = roped(q)
    k_r = roped(k)
    own = (k[:n_p].reshape(bp, tp, GQA_KV_HEADS, GQA_HEAD_DIM),
           v[:n_p].reshape(bp, tp, GQA_KV_HEADS, GQA_HEAD_DIM))
    c_k, c_v = caches
    k_keys = _keys(c_k, k_r[n_p:].reshape(bs * ts, -1), k_r[:n_p].reshape(n_p, -1)).astype(BF16)
    v_keys = _keys(c_v, v[n_p:], v[:n_p]).astype(BF16)
    rep = GQA_Q_HEADS // GQA_KV_HEADS
    o = _attend_groups(
        ([(q_r.reshape(n, -1).astype(BF16), lambda hh: hh)],
         [(k_keys, lambda hh: hh // rep)],
         (v_keys, lambda hh: hh // rep)),
        dims=dims, heads=GQA_Q_HEADS, dv=GQA_HEAD_DIM, scale=GQA_HEAD_DIM ** -0.5, out_dtype=BF16)
    return mm(o, w_out.astype(BF16)), own


def kernel(x_prompt, x_sample, cache_mla_ckv, cache_mla_krope, cache_diff_k, cache_diff_v, cache_gqa_k, cache_gqa_v, c, c_ctx, mod_w, mod_b, norm_gains, even_w_in, mla_q_norm, mla_w_uq, mla_kv_norm, mla_w_ukv, diff_lambda, diff_subln, even_w_out, odd_w_in, gqa_q_norm, gqa_k_norm, odd_w_out, router_w, router_b, moe_w_gu, moe_b_gu, moe_w_down, moe_b_down):
    bp, tp, d = x_prompt.shape
    bs, ts, _ = x_sample.shape
    past = cache_mla_ckv.shape[2]
    depth = mod_w.shape[0]
    n_p = bp * tp
    dims = (bp, tp, bs, ts, past)
    x = jnp.concatenate([x_prompt.reshape(n_p, d), x_sample.reshape(bs * ts, d)], axis=0)
    cond = jnp.concatenate([c_ctx[None, :], c], axis=0)
    n_seg = cond.shape[0]
    cond_act = jnp.pad(jax.nn.silu(cond), ((0, (-n_seg) % 16), (0, 0))).astype(BF16)
    ropes = (_rope_tables(ts, MLA_ROPE_DIM), _rope_tables(ts, DIFF_HEAD_DIM), _rope_tables(ts, GQA_HEAD_DIM))

    even_states, odd_states = [], []
    for layer in range(depth):
        mod = mm(cond_act, mod_w, w_index=layer)[:n_seg] + mod_b[layer]
        sh_m, sc_m, g_m, sh_f, sc_f, g_f = jnp.split(mod, 6, axis=-1)
        gains = norm_gains[layer]
        h = norm_mod(x, gains[0], sc_m, sh_m, n_p, ts, BF16)
        i = layer // 2
        if layer % 2 == 0:
            caches = (cache_mla_ckv[:, i], cache_mla_krope[:, i], cache_diff_k[:, i], cache_diff_v[:, i])
            lambda_init = 0.8 - 0.6 * math.exp(-0.3 * layer)
            m, own = even_mixer(h, dims, ropes, caches, even_w_in[i], mla_q_norm[i], mla_w_uq[i],
                                mla_kv_norm[i], mla_w_ukv[i], diff_lambda[i], diff_subln[i], even_w_out[i],
                                lambda_init)
            even_states.append(own)
        else:
            caches = (cache_gqa_k[:, i], cache_gqa_v[:, i])
            m, own = odd_mixer(h, dims, ropes, caches, odd_w_in[i], gqa_q_norm[i], gqa_k_norm[i], odd_w_out[i])
            odd_states.append(own)
        x = resid(x, m, gains[1], g_m, n_p, ts)
        h = norm_mod(x, gains[2], sc_f, sh_f, n_p, ts, F32)
        x = moe_block(x, h, router_w[layer], router_b[layer], moe_w_gu, moe_b_gu, moe_w_down, moe_b_down,
                      layer, gains[3], g_f, n_p, ts)

    y_prompt = x[:n_p].reshape(bp, tp, d)
    y_sample = x[n_p:].reshape(bs, ts, d)
    return (y_prompt, y_sample,
            jnp.stack([s[0] for s in even_states], axis=1),
            jnp.stack([s[1] for s in even_states], axis=1),
            jnp.stack([s[2] for s in even_states], axis=1),
            jnp.stack([s[3] for s in even_states], axis=1),
            jnp.stack([s[0] for s in odd_states], axis=1),
            jnp.stack([s[1] for s in odd_states], axis=1))
```

```python
import functools
import math

import jax
import jax.numpy as jnp
import numpy as np
from jax import lax
from jax.experimental import pallas as pl
from jax.experimental.pallas import tpu as pltpu

F32 = jnp.float32
BF16 = jnp.bfloat16

GRID_W = 64
ROPE_THETA = 10000.0
RMS_EPS = 1e-6
MLA_HEADS = 16
MLA_Q_RANK = 1024
MLA_KV_RANK = 512
MLA_NOPE_DIM = 128
MLA_ROPE_DIM = 64
MLA_V_DIM = 128
DIFF_HEADS = 8
DIFF_HEAD_DIM = 128
DIFF_V_DIM = 2 * DIFF_HEAD_DIM
DIFF_QK_WIDTH = DIFF_HEADS * 2 * DIFF_HEAD_DIM
GQA_Q_HEADS = 32
GQA_KV_HEADS = 8
GQA_HEAD_DIM = 128
N_EXPERTS = 32
TOP_K = 4
SWIGLU_LIMIT = 7.0
SWIGLU_ALPHA = 1.702

LANE = 128
ATTN_TQ = 1024
ATTN_SUB = 256
VMEM_LIMIT = 56 * 1024 * 1024


def _tile(dim, pref):
    t = min(pref, dim)
    while dim % t:
        t //= 2
    return t


def _params(sem):
    return pltpu.CompilerParams(dimension_semantics=sem, vmem_limit_bytes=VMEM_LIMIT)


def _mm_kernel(x_ref, w_ref, o_ref):
    o_ref[...] = jnp.dot(x_ref[...].astype(BF16), w_ref[...].astype(BF16),
                         preferred_element_type=F32).astype(o_ref.dtype)


def mm(x, w, out_dtype=F32, tm=1024, tn=512, w_index=None, rows=None):
    m, k = x.shape
    m = m if rows is None else rows
    n = w.shape[-1]
    tm = _tile(m, tm)
    tn = _tile(n, tn)
    if w_index is None:
        w_spec = pl.BlockSpec((k, tn), lambda i, j: (0, j))
    else:
        w_spec = pl.BlockSpec((None, k, tn), lambda i, j: (w_index, 0, j))
    return pl.pallas_call(
        _mm_kernel,
        grid=(m // tm, n // tn),
        in_specs=[pl.BlockSpec((tm, k), lambda i, j: (i, 0)), w_spec],
        out_specs=pl.BlockSpec((tm, tn), lambda i, j: (i, j)),
        out_shape=jax.ShapeDtypeStruct((m, n), out_dtype),
        compiler_params=_params(("parallel", "parallel")),
        name="mm",
    )(x, w)


def _deinterleave_src(blk):
    j = np.arange(LANE)
    o = j % blk
    return (j // blk) * blk + np.where(o < blk // 2, 2 * o, 2 * (o - blk // 2) + 1)


def _deinterleave_matrix(blk):
    p = np.zeros((LANE, LANE), np.float32)
    p[_deinterleave_src(blk), np.arange(LANE)] = 1.0
    return jnp.asarray(p, BF16)


def _permute_kernel(w_ref, p_ref, o_ref):
    o_ref[...] = jnp.dot(w_ref[...].astype(BF16), p_ref[...], preferred_element_type=F32).astype(o_ref.dtype)


def permute_lanes(w, perm):
    r, c = w.shape
    tm = _tile(r, 1024)
    return pl.pallas_call(
        _permute_kernel,
        grid=(r // tm, c // LANE),
        in_specs=[pl.BlockSpec((tm, LANE), lambda i, j: (i, j)),
                  pl.BlockSpec((LANE, LANE), lambda i, j: (0, 0))],
        out_specs=pl.BlockSpec((tm, LANE), lambda i, j: (i, j)),
        out_shape=jax.ShapeDtypeStruct((r, c), BF16),
        compiler_params=_params(("parallel", "parallel")),
        name="permute_lanes",
    )(w, perm)


def _seg_of_block(i, tm, n_prompt, rows_per_sample):
    pb = n_prompt // tm
    return jnp.where(i < pb, 0, 1 + (i - pb) // (rows_per_sample // tm))


def _norm_mod_kernel(x_ref, gain_ref, sc_ref, sh_ref, o_ref):
    x = x_ref[...]
    y = x * lax.rsqrt(jnp.mean(x * x, axis=-1, keepdims=True) + RMS_EPS) * gain_ref[...]
    o_ref[...] = (y * (1.0 + sc_ref[...]) + sh_ref[...]).astype(o_ref.dtype)


def norm_mod(x, gain, scale, shift, n_prompt, rows_per_sample, out_dtype):
    n, d = x.shape
    tm = _tile(math.gcd(n_prompt, rows_per_sample), 256)
    seg = functools.partial(_seg_of_block, tm=tm, n_prompt=n_prompt, rows_per_sample=rows_per_sample)
    s = scale.shape[0]
    return pl.pallas_call(
        _norm_mod_kernel,
        grid=(n // tm,),
        in_specs=[pl.BlockSpec((tm, d), lambda i: (i, 0)),
                  pl.BlockSpec((1, d), lambda i: (0, 0)),
                  pl.BlockSpec((None, 1, d), lambda i: (seg(i), 0, 0)),
                  pl.BlockSpec((None, 1, d), lambda i: (seg(i), 0, 0))],
        out_specs=pl.BlockSpec((tm, d), lambda i: (i, 0)),
        out_shape=jax.ShapeDtypeStruct((n, d), out_dtype),
        compiler_params=_params(("parallel",)),
        name="norm_mod",
    )(x, gain.reshape(1, d), scale.reshape(s, 1, d), shift.reshape(s, 1, d))


def _resid_kernel(x_ref, m_ref, gain_ref, g_ref, o_ref):
    m = m_ref[...]
    y = m * lax.rsqrt(jnp.mean(m * m, axis=-1, keepdims=True) + RMS_EPS) * gain_ref[...]
    o_ref[...] = x_ref[...] + g_ref[...] * y


def resid(x, m, gain, gate, n_prompt, rows_per_sample):
    n, d = x.shape
    tm = _tile(math.gcd(n_prompt, rows_per_sample), 256)
    seg = functools.partial(_seg_of_block, tm=tm, n_prompt=n_prompt, rows_per_sample=rows_per_sample)
    s = gate.shape[0]
    return pl.pallas_call(
        _resid_kernel,
        grid=(n // tm,),
        in_specs=[pl.BlockSpec((tm, d), lambda i: (i, 0)),
                  pl.BlockSpec((tm, d), lambda i: (i, 0)),
                  pl.BlockSpec((1, d), lambda i: (0, 0)),
                  pl.BlockSpec((None, 1, d), lambda i: (seg(i), 0, 0))],
        out_specs=pl.BlockSpec((tm, d), lambda i: (i, 0)),
        out_shape=jax.ShapeDtypeStruct((n, d), F32),
        compiler_params=_params(("parallel",)),
        name="resid",
    )(x, m, gain.reshape(1, d), gate.reshape(s, 1, d))


def _attn_kernel(*refs, n_q, n_k, exp2_scale):
    q_refs = refs[:n_q]
    k_refs = refs[n_q:n_q + n_k]
    v_ref = refs[n_q + n_k]
    o_ref = refs[n_q + n_k + 1]

    if n_k > 1:
        kcat = refs[n_q + n_k + 2]

        @pl.when(pl.program_id(2) == 0)
        def _():
            for idx, r in enumerate(k_refs):
                kcat[:, idx * LANE:(idx + 1) * LANE] = r[...]
        k = kcat[...]
    else:
        k = k_refs[0][...]
    v = v_ref[...]
    tq = o_ref.shape[0]
    sub = min(tq, ATTN_SUB)
    for r0 in range(0, tq, sub):
        qs = [r[r0:r0 + sub, :] for r in q_refs]
        q = jnp.concatenate(qs, axis=1) if n_q > 1 else qs[0]
        s = lax.dot_general(q, k, (((1,), (1,)), ((), ())), preferred_element_type=F32)
        m = jnp.max(s, axis=1, keepdims=True)
        p = jnp.exp2((s - m) * exp2_scale)
        l = jnp.sum(p, axis=1, keepdims=True)
        o = jnp.dot(p.astype(BF16), v, preferred_element_type=F32)
        o_ref[r0:r0 + sub, :] = (o / l).astype(o_ref.dtype)


def attention(q_parts, k_parts, v_part, *, batch, heads, tq_total, tk_total, q_row0, k_row0,
              dv, scale, out_dtype):
    tq = _tile(tq_total, ATTN_TQ)
    nq = tq_total // tq
    assert q_row0 % tq == 0 and k_row0 % tk_total == 0
    qb0, kb0 = q_row0 // tq, k_row0 // tk_total

    def q_spec(fn):
        return pl.BlockSpec((tq, LANE), lambda b, h, i: (qb0 + b * nq + i, fn(h)))

    def k_spec(fn, width):
        return pl.BlockSpec((tk_total, width), lambda b, h, i: (kb0 + b, fn(h)))

    in_specs = ([q_spec(fn) for _, fn in q_parts] + [k_spec(fn, LANE) for _, fn in k_parts]
                + [k_spec(v_part[1], dv)])
    n_k = len(k_parts)
    kern = functools.partial(_attn_kernel, n_q=len(q_parts), n_k=n_k, exp2_scale=scale * math.log2(math.e))
    scratch = [pltpu.VMEM((tk_total, n_k * LANE), BF16)] if n_k > 1 else []
    return pl.pallas_call(
        kern,
        grid=(batch, heads, nq),
        in_specs=in_specs,
        out_specs=pl.BlockSpec((tq, dv), lambda b, h, i: (b * nq + i, h)),
        out_shape=jax.ShapeDtypeStruct((batch * tq_total, heads * dv), out_dtype),
        scratch_shapes=scratch,
        compiler_params=_params(("parallel", "parallel", "arbitrary")),
        name="attention",
    )(*[a for a, _ in q_parts], *[a for a, _ in k_parts], v_part[0])


MOE_TM = 512
MOE_GATHER_TM = 256
MOE_COMBINE_TM = 128


def _moe_gather_kernel(tok_ref, nused_ref, h_ref, o_ref, buf, sem, *, tm, blocks_per_moe_block):
    b = pl.program_id(0)
    used = b < nused_ref[0] * blocks_per_moe_block

    @pl.when(used)
    def _():
        def issue(r, carry):
            pltpu.make_async_copy(h_ref.at[pl.ds(tok_ref[r], 1), :], buf.at[pl.ds(r, 1), :], sem).start()
            return carry
        lax.fori_loop(0, tm, issue, 0)
        pltpu.make_async_copy(h_ref.at[pl.ds(0, tm), :], buf, sem).wait()
        o_ref[...] = buf[...].astype(o_ref.dtype)

    @pl.when(jnp.logical_not(used))
    def _():
        o_ref[...] = jnp.zeros(o_ref.shape, o_ref.dtype)


def moe_gather(h, slot_tok, nused, n_slots):
    n, d = h.shape
    tm = MOE_GATHER_TM
    nb = n_slots // tm
    kern = functools.partial(_moe_gather_kernel, tm=tm, blocks_per_moe_block=MOE_TM // tm)
    return pl.pallas_call(
        kern,
        grid=(nb,),
        in_specs=[pl.BlockSpec((None, None, tm), lambda b: (b, 0, 0), memory_space=pltpu.SMEM),
                  pl.BlockSpec(memory_space=pltpu.SMEM),
                  pl.BlockSpec(memory_space=pl.ANY)],
        out_specs=pl.BlockSpec((tm, d), lambda b: (b, 0)),
        out_shape=jax.ShapeDtypeStruct((n_slots, d), BF16),
        scratch_shapes=[pltpu.VMEM((tm, d), F32), pltpu.SemaphoreType.DMA(())],
        compiler_params=_params(("arbitrary",)),
        name="moe_gather",
    )(slot_tok.reshape(nb, 1, tm), nused, h)


def _expert_changed(be_ref, b):
    return jnp.logical_or(b == 0, be_ref[b] != be_ref[jnp.maximum(b - 1, 0)])


def _moe_gu_kernel(be_ref, nused_ref, x_ref, wg_ref, wu_ref, bg_ref, bu_ref, o_ref, wg_bf, wu_bf):
    b = pl.program_id(1)

    @pl.when(_expert_changed(be_ref, b))
    def _():
        wg_bf[...] = wg_ref[...].astype(BF16)
        wu_bf[...] = wu_ref[...].astype(BF16)

    @pl.when(b < nused_ref[0])
    def _():
        x = x_ref[...]
        g = jnp.dot(x, wg_bf[...], preferred_element_type=F32) + bg_ref[...]
        u = jnp.dot(x, wu_bf[...], preferred_element_type=F32) + bu_ref[...]
        gate_h = jnp.minimum(g, SWIGLU_LIMIT)
        up_h = jnp.clip(u, -SWIGLU_LIMIT, SWIGLU_LIMIT)
        o_ref[...] = ((up_h + 1.0) * gate_h * jax.nn.sigmoid(SWIGLU_ALPHA * gate_h)).astype(o_ref.dtype)

    @pl.when(b >= nused_ref[0])
    def _():
        o_ref[...] = jnp.zeros(o_ref.shape, o_ref.dtype)


def moe_gate_up(xs, w_gu, b_gu, layer, block_expert, nused):
    n_slots, d = xs.shape
    depth, n_e, _, two_ff = w_gu.shape
    d_ff = two_ff // 2
    tm = MOE_TM
    tn = _tile(d_ff, 256)
    nj = d_ff // tn
    nb = n_slots // tm
    grid_spec = pltpu.PrefetchScalarGridSpec(
        num_scalar_prefetch=2,
        grid=(nj, nb),
        in_specs=[pl.BlockSpec((tm, d), lambda j, b, be, nu: (b, 0)),
                  pl.BlockSpec((None, None, d, tn), lambda j, b, be, nu: (layer, be[b], 0, j)),
                  pl.BlockSpec((None, None, d, tn), lambda j, b, be, nu: (layer, be[b], 0, nj + j)),
                  pl.BlockSpec((None, None, 1, tn), lambda j, b, be, nu: (layer, be[b], 0, j)),
                  pl.BlockSpec((None, None, 1, tn), lambda j, b, be, nu: (layer, be[b], 0, nj + j))],
        out_specs=pl.BlockSpec((tm, tn), lambda j, b, be, nu: (b, j)),
        scratch_shapes=[pltpu.VMEM((d, tn), BF16), pltpu.VMEM((d, tn), BF16)],
    )
    b3 = b_gu.reshape(depth, n_e, 1, two_ff)
    return pl.pallas_call(
        _moe_gu_kernel,
        grid_spec=grid_spec,
        out_shape=jax.ShapeDtypeStruct((n_slots, d_ff), BF16),
        compiler_params=_params(("arbitrary", "arbitrary")),
        name="moe_gate_up",
    )(block_expert, nused, xs, w_gu, w_gu, b3, b3)


def _moe_down_kernel(be_ref, nused_ref, h_ref, w_ref, bias_ref, o_ref, w_bf):
    b = pl.program_id(1)

    @pl.when(_expert_changed(be_ref, b))
    def _():
        w_bf[...] = w_ref[...].astype(BF16)

    @pl.when(b < nused_ref[0])
    def _():
        o_ref[...] = jnp.dot(h_ref[...], w_bf[...], preferred_element_type=F32) + bias_ref[...]

    @pl.when(b >= nused_ref[0])
    def _():
        o_ref[...] = jnp.zeros(o_ref.shape, o_ref.dtype)


def moe_down(hdn, w_down, b_down, layer, block_expert, nused):
    n_slots, d_ff = hdn.shape
    depth, n_e, _, d = w_down.shape
    tm = MOE_TM
    tn = _tile(d, 1024)
    nb = n_slots // tm
    grid_spec = pltpu.PrefetchScalarGridSpec(
        num_scalar_prefetch=2,
        grid=(d // tn, nb),
        in_specs=[pl.BlockSpec((tm, d_ff), lambda j, b, be, nu: (b, 0)),
                  pl.BlockSpec((None, None, d_ff, tn), lambda j, b, be, nu: (layer, be[b], 0, j)),
                  pl.BlockSpec((None, None, 1, tn), lambda j, b, be, nu: (layer, be[b], 0, j))],
        out_specs=pl.BlockSpec((tm, tn), lambda j, b, be, nu: (b, j)),
        scratch_shapes=[pltpu.VMEM((d_ff, tn), BF16)],
    )
    return pl.pallas_call(
        _moe_down_kernel,
        grid_spec=grid_spec,
        out_shape=jax.ShapeDtypeStruct((n_slots, d), F32),
        compiler_params=_params(("arbitrary", "arbitrary")),
        name="moe_down",
    )(block_expert, nused, hdn, w_down, b_down.reshape(depth, n_e, 1, d))


def _moe_combine_kernel(pos_ref, gates_ref, x_ref, y_ref, gain_ref, g_ref, o_ref, buf, sem, *, tm):
    def issue(r, carry):
        for k in range(TOP_K):
            pltpu.make_async_copy(y_ref.at[pl.ds(pos_ref[r * TOP_K + k], 1), :],
                                  buf.at[k, pl.ds(r, 1), :], sem).start()
        return carry
    lax.fori_loop(0, tm, issue, 0)
    for k in range(TOP_K):
        pltpu.make_async_copy(y_ref.at[pl.ds(0, tm), :], buf.at[k], sem).wait()
    gates = gates_ref[...]
    f = buf[0] * gates[:, 0:1]
    for k in range(1, TOP_K):
        f = f + buf[k] * gates[:, k:k + 1]
    y = f * lax.rsqrt(jnp.mean(f * f, axis=-1, keepdims=True) + RMS_EPS) * gain_ref[...]
    o_ref[...] = x_ref[...] + g_ref[...] * y


def moe_combine_resid(x, y, pos, gates, gain, gate_mod, n_prompt, rows_per_sample):
    n, d = x.shape
    tm = _tile(math.gcd(n_prompt, rows_per_sample), MOE_COMBINE_TM)
    seg = functools.partial(_seg_of_block, tm=tm, n_prompt=n_prompt, rows_per_sample=rows_per_sample)
    s = gate_mod.shape[0]
    nb = n // tm
    kern = functools.partial(_moe_combine_kernel, tm=tm)
    return pl.pallas_call(
        kern,
        grid=(nb,),
        in_specs=[pl.BlockSpec((None, None, tm * TOP_K), lambda i: (i, 0, 0), memory_space=pltpu.SMEM),
                  pl.BlockSpec((tm, TOP_K), lambda i: (i, 0)),
                  pl.BlockSpec((tm, d), lambda i: (i, 0)),
                  pl.BlockSpec(memory_space=pl.ANY),
                  pl.BlockSpec((1, d), lambda i: (0, 0)),
                  pl.BlockSpec((None, 1, d), lambda i: (seg(i), 0, 0))],
        out_specs=pl.BlockSpec((tm, d), lambda i: (i, 0)),
        out_shape=jax.ShapeDtypeStruct((n, d), F32),
        scratch_shapes=[pltpu.VMEM((TOP_K, tm, d), F32), pltpu.SemaphoreType.DMA(())],
        compiler_params=_params(("arbitrary",)),
        name="moe_combine",
    )(pos.reshape(nb, 1, tm * TOP_K), gates, x, y, gain.reshape(1, d), gate_mod.reshape(s, 1, d))


def moe_route(h, w_r, b_r):
    n, d = h.shape
    w_pad = jnp.pad(w_r, ((0, 0), (0, LANE - N_EXPERTS))).astype(BF16)
    logits = mm(h, w_pad, tm=512, tn=LANE)[:, :N_EXPERTS] + b_r
    top_logit, top_idx = lax.top_k(logits, TOP_K)
    gates = jax.nn.softmax(top_logit, axis=-1)
    a = n * TOP_K
    flat_e = top_idx.reshape(a)
    onehot = (flat_e[:, None] == jnp.arange(N_EXPERTS, dtype=flat_e.dtype)[None, :]).astype(jnp.int32)
    csum = jnp.cumsum(onehot, axis=0)
    rank = jnp.take_along_axis(csum, flat_e[:, None], axis=1)[:, 0] - 1
    counts = csum[-1]
    padded = (counts + MOE_TM - 1) // MOE_TM * MOE_TM
    pad_end = jnp.cumsum(padded)
    pad_start = pad_end - padded
    dest = (pad_start[flat_e] + rank).astype(jnp.int32)
    n_blocks = a // MOE_TM + N_EXPERTS
    n_slots = n_blocks * MOE_TM
    slot_tok = jnp.zeros((n_slots,), jnp.int32).at[dest].set(jnp.arange(a, dtype=jnp.int32) // TOP_K)
    block_expert = jnp.minimum(
        jnp.searchsorted(pad_end, jnp.arange(n_blocks, dtype=jnp.int32) * MOE_TM, side='right'),
        N_EXPERTS - 1).astype(jnp.int32)
    nused = (pad_end[-1] // MOE_TM).astype(jnp.int32).reshape(1)
    return slot_tok, block_expert, nused, dest.reshape(n, TOP_K), gates, n_slots


def moe_block(x, h, w_r, b_r, w_gu, b_gu, w_down, b_down, layer, gain, gate_mod, n_prompt, rows_per_sample):
    slot_tok, block_expert, nused, pos, gates, n_slots = moe_route(h, w_r, b_r)
    xs = moe_gather(h, slot_tok, nused, n_slots)
    hdn = moe_gate_up(xs, w_gu, b_gu, layer, block_expert, nused)
    y = moe_down(hdn, w_down, b_down, layer, block_expert, nused)
    return moe_combine_resid(x, y, pos, gates, gain, gate_mod, n_prompt, rows_per_sample)


def _rms(x, gain):
    return x * lax.rsqrt(jnp.mean(x * x, axis=-1, keepdims=True) + RMS_EPS) * gain


def _rope_tables(n_tokens, rot_dim):
    rows = n_tokens // GRID_W
    row = jnp.repeat(jnp.arange(rows, dtype=F32), GRID_W)
    col = jnp.tile(jnp.arange(GRID_W, dtype=F32), rows)
    axis_dim = rot_dim // 2
    inv_freq = ROPE_THETA ** (-jnp.arange(0, axis_dim, 2, dtype=F32) / axis_dim)
    ang = jnp.concatenate([row[:, None] * inv_freq, col[:, None] * inv_freq], axis=-1)
    cos, sin = jnp.cos(ang), jnp.sin(ang)
    return jnp.concatenate([cos, cos], axis=-1), jnp.concatenate([-sin, sin], axis=-1)


def _rope(x, c_tab, s_tab):
    shape = (c_tab.shape[0],) + (1,) * (x.ndim - 3) + (c_tab.shape[1],)
    return x * c_tab.reshape(shape) + jnp.roll(x, x.shape[-1] // 2, axis=-1) * s_tab.reshape(shape)


def _rope_sample_rows(x, tabs, n_p, bs, ts):
    xs = x[n_p:].reshape((bs, ts) + x.shape[1:])
    return jnp.concatenate([x[:n_p], _rope(xs, *tabs).reshape((bs * ts,) + x.shape[1:])], axis=0)


def _keys(bs, cache, own_sample, own_prompt):
    w = own_prompt.shape[-1]
    s = jnp.concatenate([cache.reshape(bs, -1, w), own_sample.reshape(bs, -1, w)], axis=1)
    return jnp.concatenate([s.reshape(-1, w), own_prompt], axis=0)


def _attend_groups(make_parts, *, dims, heads, dv, scale, out_dtype):
    bp, tp, bs, ts, past = dims
    q_parts, k_parts, v_part = make_parts
    o_p = attention(q_parts, k_parts, v_part, batch=bp, heads=heads, tq_total=tp, tk_total=tp,
                    q_row0=0, k_row0=bs * (past + ts), dv=dv, scale=scale, out_dtype=out_dtype)
    o_s = attention(q_parts, k_parts, v_part, batch=bs, heads=heads, tq_total=ts, tk_total=past + ts,
                    q_row0=bp * tp, k_row0=0, dv=dv, scale=scale, out_dtype=out_dtype)
    return jnp.concatenate([o_p, o_s], axis=0)


def even_mixer(h, dims, ropes, caches, w_in, q_norm, w_uq, kv_norm, w_ukv, lam_params, subln, w_out,
               lambda_init):
    bp, tp, bs, ts, past = dims
    n_p = bp * tp
    n = h.shape[0]
    c1 = MLA_Q_RANK + MLA_KV_RANK
    c2 = c1 + MLA_ROPE_DIM
    c3 = c2 + 2 * DIFF_QK_WIDTH
    p128, p64 = _deinterleave_matrix(LANE), _deinterleave_matrix(MLA_ROPE_DIM)
    w_in_bf = w_in.astype(BF16)
    a = mm(h, w_in_bf[:, :c1])
    w_kr = jnp.pad(w_in[:, c1:c2], ((0, 0), (0, LANE - MLA_ROPE_DIM)))
    kr = mm(h, permute_lanes(w_kr, p64), tn=LANE)[:, :MLA_ROPE_DIM]
    kr_own = mm(h, w_kr.astype(BF16), tn=LANE, rows=n_p)[:, :MLA_ROPE_DIM]
    df = mm(h, jnp.concatenate([permute_lanes(w_in[:, c2:c3], p128), w_in_bf[:, c3:]], axis=1))
    dk_own = mm(h, w_in_bf[:, c2 + DIFF_QK_WIDTH:c3], rows=n_p)
    cq, ckv = a[:, :MLA_Q_RANK], a[:, MLA_Q_RANK:]
    w_uq3 = w_uq.reshape(MLA_Q_RANK, MLA_HEADS, MLA_NOPE_DIM + MLA_ROPE_DIM)
    w_uq_perm = jnp.concatenate(
        [w_uq3[:, :, :MLA_NOPE_DIM].reshape(MLA_Q_RANK, -1).astype(BF16),
         permute_lanes(w_uq3[:, :, MLA_NOPE_DIM:].reshape(MLA_Q_RANK, -1), p64)], axis=1)
    q = mm(_rms(cq, q_norm).astype(BF16), w_uq_perm)
    n_nope = MLA_HEADS * MLA_NOPE_DIM
    q_nope = q[:, :n_nope].astype(BF16)
    q_rope = q[:, n_nope:].reshape(n, MLA_HEADS, MLA_ROPE_DIM)
    ckv = _rms(ckv, kv_norm)
    dq = df[:, :DIFF_QK_WIDTH].reshape(n, DIFF_HEADS, 2, DIFF_HEAD_DIM)
    dk = df[:, DIFF_QK_WIDTH:2 * DIFF_QK_WIDTH].reshape(n, DIFF_HEADS, 2, DIFF_HEAD_DIM)
    dv = df[:, 2 * DIFF_QK_WIDTH:]

    q_rope = _rope_sample_rows(q_rope, ropes[0], n_p, bs, ts)
    kr_r = _rope_sample_rows(kr, ropes[0], n_p, bs, ts).astype(BF16)
    dq = _rope_sample_rows(dq, ropes[1], n_p, bs, ts)
    dk_r = _rope_sample_rows(dk, ropes[1], n_p, bs, ts).reshape(n, -1).astype(BF16)

    own = (ckv[:n_p].reshape(bp, tp, MLA_KV_RANK), kr_own.reshape(bp, tp, MLA_ROPE_DIM),
           dk_own.reshape(bp, tp, DIFF_HEADS, 2 * DIFF_HEAD_DIM),
           dv[:n_p].reshape(bp, tp, DIFF_HEADS, DIFF_V_DIM))

    c_ckv, c_kr, c_dk, c_dv = caches
    ckv_bf = ckv.astype(BF16)
    ckv_keys = _keys(bs, c_ckv.astype(BF16), ckv_bf[n_p:], ckv_bf[:n_p])
    kv = mm(ckv_keys, w_ukv.astype(BF16), out_dtype=BF16)
    c_kr_perm = c_kr[..., _deinterleave_src(MLA_ROPE_DIM)[:MLA_ROPE_DIM]].astype(BF16)
    kr_keys = jnp.pad(_keys(bs, c_kr_perm, kr_r[n_p:], kr_r[:n_p]), ((0, 0), (0, LANE - MLA_ROPE_DIM)))
    q_rope_pad = jnp.pad(q_rope, ((0, 0), (0, 0), (0, LANE - MLA_ROPE_DIM))).reshape(n, -1).astype(BF16)
    a_out = _attend_groups(
        ([(q_nope, lambda hh: hh), (q_rope_pad, lambda hh: hh)],
         [(kv, lambda hh: 2 * hh), (kr_keys, lambda hh: 0)],
         (kv, lambda hh: 2 * hh + 1)),
        dims=dims, heads=MLA_HEADS, dv=MLA_V_DIM,
        scale=(MLA_NOPE_DIM + MLA_ROPE_DIM) ** -0.5, out_dtype=BF16)

    c_dk_perm = permute_lanes(c_dk.reshape(bs * past, -1), p128)
    dk_keys = _keys(bs, c_dk_perm, dk_r[n_p:], dk_r[:n_p])
    dv_bf = dv.astype(BF16)
    dv_keys = _keys(bs, c_dv.astype(BF16), dv_bf[n_p:], dv_bf[:n_p])
    o2 = _attend_groups(
        ([(dq.reshape(n, -1).astype(BF16), lambda hh: hh)],
         [(dk_keys, lambda hh: hh)],
         (dv_keys, lambda hh: hh // 2)),
        dims=dims, heads=2 * DIFF_HEADS, dv=DIFF_V_DIM,
        scale=DIFF_HEAD_DIM ** -0.5, out_dtype=F32).reshape(n, DIFF_HEADS, 2, DIFF_V_DIM)
    lq1, lk1, lq2, lk2 = lam_params.astype(F32)
    lam = jnp.exp(jnp.sum(lq1 * lk1)) - jnp.exp(jnp.sum(lq2 * lk2)) + lambda_init
    b_out = o2[:, :, 0] - lam * o2[:, :, 1]
    b_out = _rms(b_out, subln) * (1.0 - lambda_init)
    merged = jnp.concatenate([a_out, b_out.reshape(n, -1).astype(BF16)], axis=-1)
    return mm(merged, w_out.astype(BF16)), own


def odd_mixer(h, dims, ropes, caches, w_in, q_norm, k_norm, w_out):
    bp, tp, bs, ts, past = dims
    n_p = bp * tp
    n = h.shape[0]
    nq = GQA_Q_HEADS * GQA_HEAD_DIM
    nk = GQA_KV_HEADS * GQA_HEAD_DIM
    p128 = _deinterleave_matrix(LANE)
    src = _deinterleave_src(LANE)
    w_in_bf = w_in.astype(BF16)
    qkv = mm(h, jnp.concatenate([permute_lanes(w_in[:, :nq + nk], p128), w_in_bf[:, nq + nk:]], axis=1))
    k_own = _rms(mm(h, w_in_bf[:, nq:nq + nk], rows=n_p).reshape(n_p, GQA_KV_HEADS, GQA_HEAD_DIM), k_norm)
    q = _rms(qkv[:, :nq].reshape(n, GQA_Q_HEADS, GQA_HEAD_DIM), q_norm[src])
    k = _rms(qkv[:, nq:nq + nk].reshape(n, GQA_KV_HEADS, GQA_HEAD_DIM), k_norm[src])
    v = qkv[:, nq + nk:]
    q_r = _rope_sample_rows(q, ropes[2], n_p, bs, ts).reshape(n, -1).astype(BF16)
    k_r = _rope_sample_rows(k, ropes[2], n_p, bs, ts).reshape(n, -1).astype(BF16)
    own = (k_own.reshape(bp, tp, GQA_KV_HEADS, GQA_HEAD_DIM),
           v[:n_p].reshape(bp, tp, GQA_KV_HEADS, GQA_HEAD_DIM))
    c_k, c_v = caches
    k_keys = _keys(bs, permute_lanes(c_k.reshape(bs * past, -1), p128), k_r[n_p:], k_r[:n_p])
    v_bf = v.astype(BF16)
    v_keys = _keys(bs, c_v.astype(BF16), v_bf[n_p:], v_bf[:n_p])
    rep = GQA_Q_HEADS // GQA_KV_HEADS
    o = _attend_groups(
        ([(q_r, lambda hh: hh)],
         [(k_keys, lambda hh: hh // rep)],
         (v_keys, lambda hh: hh // rep)),
        dims=dims, heads=GQA_Q_HEADS, dv=GQA_HEAD_DIM, scale=GQA_HEAD_DIM ** -0.5, out_dtype=BF16)
    return mm(o, w_out.astype(BF16)), own


def kernel(x_prompt, x_sample, cache_mla_ckv, cache_mla_krope, cache_diff_k, cache_diff_v, cache_gqa_k, cache_gqa_v, c, c_ctx, mod_w, mod_b, norm_gains, even_w_in, mla_q_norm, mla_w_uq, mla_kv_norm, mla_w_ukv, diff_lambda, diff_subln, even_w_out, odd_w_in, gqa_q_norm, gqa_k_norm, odd_w_out, router_w, router_b, moe_w_gu, moe_b_gu, moe_w_down, moe_b_down):
    bp, tp, d = x_prompt.shape
    bs, ts, _ = x_sample.shape
    past = cache_mla_ckv.shape[2]
    depth = mod_w.shape[0]
    n_p = bp * tp
    dims = (bp, tp, bs, ts, past)
    x = jnp.concatenate([x_prompt.reshape(n_p, d), x_sample.reshape(bs * ts, d)], axis=0)
    cond = jnp.concatenate([c_ctx[None, :], c], axis=0)
    n_seg = cond.shape[0]
    cond_act = jnp.pad(jax.nn.silu(cond), ((0, (-n_seg) % 16), (0, 0))).astype(BF16)
    ropes = (_rope_tables(ts, MLA_ROPE_DIM), _rope_tables(ts, DIFF_HEAD_DIM), _rope_tables(ts, GQA_HEAD_DIM))

    even_states, odd_states = [], []
    for layer in range(depth):
        mod = mm(cond_act, mod_w, w_index=layer)[:n_seg] + mod_b[layer]
        sh_m, sc_m, g_m, sh_f, sc_f, g_f = jnp.split(mod, 6, axis=-1)
        gains = norm_gains[layer]
        h = norm_mod(x, gains[0], sc_m, sh_m, n_p, ts, BF16)
        i = layer // 2
        if layer % 2 == 0:
            caches = (cache_mla_ckv[:, i], cache_mla_krope[:, i], cache_diff_k[:, i], cache_diff_v[:, i])
            lambda_init = 0.8 - 0.6 * math.exp(-0.3 * layer)
            m, own = even_mixer(h, dims, ropes, caches, even_w_in[i], mla_q_norm[i], mla_w_uq[i],
                                mla_kv_norm[i], mla_w_ukv[i], diff_lambda[i], diff_subln[i], even_w_out[i],
                                lambda_init)
            even_states.append(own)
        else:
            caches = (cache_gqa_k[:, i], cache_gqa_v[:, i])
            m, own = odd_mixer(h, dims, ropes, caches, odd_w_in[i], gqa_q_norm[i], gqa_k_norm[i], odd_w_out[i])
            odd_states.append(own)
        x = resid(x, m, gains[1], g_m, n_p, ts)
        h = norm_mod(x, gains[2], sc_f, sh_f, n_p, ts, F32)
        x = moe_block(x, h, router_w[layer], router_b[layer], moe_w_gu, moe_b_gu, moe_w_down, moe_b_down,
                      layer, gains[3], g_f, n_p, ts)

    y_prompt = x[:n_p].reshape(bp, tp, d)
    y_sample = x[n_p:].reshape(bs, ts, d)
    return (y_prompt, y_sample,
            jnp.stack([s[0] for s in even_states], axis=1),
            jnp.stack([s[1] for s in even_states], axis=1),
            jnp.stack([s[2] for s in even_states], axis=1),
            jnp.stack([s[3] for s in even_states], axis=1),
            jnp.stack([s[0] for s in odd_states], axis=1),
            jnp.stack([s[1] for s in odd_states], axis=1))
```

```python
import functools
import math

import jax
import jax.numpy as jnp
import numpy as np
from jax import lax
from jax.experimental import pallas as pl
from jax.experimental.pallas import tpu as pltpu

F32 = jnp.float32
BF16 = jnp.bfloat16

GRID_W = 64
ROPE_THETA = 10000.0
RMS_EPS = 1e-6
MLA_HEADS = 16
MLA_Q_RANK = 1024
MLA_KV_RANK = 512
MLA_NOPE_DIM = 128
MLA_ROPE_DIM = 64
MLA_V_DIM = 128
DIFF_HEADS = 8
DIFF_HEAD_DIM = 128
DIFF_V_DIM = 2 * DIFF_HEAD_DIM
DIFF_QK_WIDTH = DIFF_HEADS * 2 * DIFF_HEAD_DIM
GQA_Q_HEADS = 32
GQA_KV_HEADS = 8
GQA_HEAD_DIM = 128
N_EXPERTS = 32
TOP_K = 4
SWIGLU_LIMIT = 7.0
SWIGLU_ALPHA = 1.702

LANE = 128
ATTN_TQ = 2048
ATTN_SUB = 256
VMEM_LIMIT = 56 * 1024 * 1024


def _tile(dim, pref):
    t = min(pref, dim)
    while dim % t:
        t //= 2
    return t


def _params(sem):
    return pltpu.CompilerParams(dimension_semantics=sem, vmem_limit_bytes=VMEM_LIMIT)


def _mm_kernel(x_ref, w_ref, o_ref):
    o_ref[...] = jnp.dot(x_ref[...].astype(BF16), w_ref[...].astype(BF16),
                         preferred_element_type=F32).astype(o_ref.dtype)


def mm(x, w, out_dtype=F32, tm=1024, tn=512, w_index=None, rows=None):
    m, k = x.shape
    m = m if rows is None else rows
    n = w.shape[-1]
    tm = _tile(m, tm)
    tn = _tile(n, tn)
    if w_index is None:
        w_spec = pl.BlockSpec((k, tn), lambda i, j: (0, j))
    else:
        w_spec = pl.BlockSpec((None, k, tn), lambda i, j: (w_index, 0, j))
    return pl.pallas_call(
        _mm_kernel,
        grid=(m // tm, n // tn),
        in_specs=[pl.BlockSpec((tm, k), lambda i, j: (i, 0)), w_spec],
        out_specs=pl.BlockSpec((tm, tn), lambda i, j: (i, j)),
        out_shape=jax.ShapeDtypeStruct((m, n), out_dtype),
        compiler_params=_params(("parallel", "parallel")),
        name="mm",
    )(x, w)


def _deinterleave_src(blk):
    j = np.arange(LANE)
    o = j % blk
    return (j // blk) * blk + np.where(o < blk // 2, 2 * o, 2 * (o - blk // 2) + 1)


def _deinterleave_matrix(blk):
    p = np.zeros((LANE, LANE), np.float32)
    p[_deinterleave_src(blk), np.arange(LANE)] = 1.0
    return jnp.asarray(p, BF16)


def _permute_kernel(w_ref, p_ref, o_ref):
    o_ref[...] = jnp.dot(w_ref[...].astype(BF16), p_ref[...], preferred_element_type=F32).astype(o_ref.dtype)


def permute_lanes(w, perm):
    r, c = w.shape
    tm = _tile(r, 1024)
    return pl.pallas_call(
        _permute_kernel,
        grid=(r // tm, c // LANE),
        in_specs=[pl.BlockSpec((tm, LANE), lambda i, j: (i, j)),
                  pl.BlockSpec((LANE, LANE), lambda i, j: (0, 0))],
        out_specs=pl.BlockSpec((tm, LANE), lambda i, j: (i, j)),
        out_shape=jax.ShapeDtypeStruct((r, c), BF16),
        compiler_params=_params(("parallel", "parallel")),
        name="permute_lanes",
    )(w, perm)


def _seg_of_block(i, tm, n_prompt, rows_per_sample):
    pb = n_prompt // tm
    return jnp.where(i < pb, 0, 1 + (i - pb) // (rows_per_sample // tm))


def _norm_mod_kernel(x_ref, gain_ref, sc_ref, sh_ref, o_ref):
    x = x_ref[...]
    y = x * lax.rsqrt(jnp.mean(x * x, axis=-1, keepdims=True) + RMS_EPS) * gain_ref[...]
    o_ref[...] = (y * (1.0 + sc_ref[...]) + sh_ref[...]).astype(o_ref.dtype)


def norm_mod(x, gain, scale, shift, n_prompt, rows_per_sample, out_dtype):
    n, d = x.shape
    tm = _tile(math.gcd(n_prompt, rows_per_sample), 256)
    seg = functools.partial(_seg_of_block, tm=tm, n_prompt=n_prompt, rows_per_sample=rows_per_sample)
    s = scale.shape[0]
    return pl.pallas_call(
        _norm_mod_kernel,
        grid=(n // tm,),
        in_specs=[pl.BlockSpec((tm, d), lambda i: (i, 0)),
                  pl.BlockSpec((1, d), lambda i: (0, 0)),
                  pl.BlockSpec((None, 1, d), lambda i: (seg(i), 0, 0)),
                  pl.BlockSpec((None, 1, d), lambda i: (seg(i), 0, 0))],
        out_specs=pl.BlockSpec((tm, d), lambda i: (i, 0)),
        out_shape=jax.ShapeDtypeStruct((n, d), out_dtype),
        compiler_params=_params(("parallel",)),
        name="norm_mod",
    )(x, gain.reshape(1, d), scale.reshape(s, 1, d), shift.reshape(s, 1, d))


def _resid_kernel(x_ref, m_ref, gain_ref, g_ref, o_ref):
    m = m_ref[...]
    y = m * lax.rsqrt(jnp.mean(m * m, axis=-1, keepdims=True) + RMS_EPS) * gain_ref[...]
    o_ref[...] = x_ref[...] + g_ref[...] * y


def resid(x, m, gain, gate, n_prompt, rows_per_sample):
    n, d = x.shape
    tm = _tile(math.gcd(n_prompt, rows_per_sample), 256)
    seg = functools.partial(_seg_of_block, tm=tm, n_prompt=n_prompt, rows_per_sample=rows_per_sample)
    s = gate.shape[0]
    return pl.pallas_call(
        _resid_kernel,
        grid=(n // tm,),
        in_specs=[pl.BlockSpec((tm, d), lambda i: (i, 0)),
                  pl.BlockSpec((tm, d), lambda i: (i, 0)),
                  pl.BlockSpec((1, d), lambda i: (0, 0)),
                  pl.BlockSpec((None, 1, d), lambda i: (seg(i), 0, 0))],
        out_specs=pl.BlockSpec((tm, d), lambda i: (i, 0)),
        out_shape=jax.ShapeDtypeStruct((n, d), F32),
        compiler_params=_params(("parallel",)),
        name="resid",
    )(x, m, gain.reshape(1, d), gate.reshape(s, 1, d))


def _attn_kernel(*refs, n_q, n_k, exp2_scale):
    q_refs = refs[:n_q]
    k_refs = refs[n_q:n_q + n_k]
    v_ref = refs[n_q + n_k]
    o_ref = refs[n_q + n_k + 1]

    if n_k > 1:
        kcat = refs[n_q + n_k + 2]

        @pl.when(pl.program_id(2) == 0)
        def _():
            for idx, r in enumerate(k_refs):
                kcat[:, idx * LANE:(idx + 1) * LANE] = r[...]
        k = kcat[...]
    else:
        k = k_refs[0][...]
    v = v_ref[...]
    tq = o_ref.shape[0]
    sub = min(tq, ATTN_SUB)
    for r0 in range(0, tq, sub):
        qs = [r[r0:r0 + sub, :] for r in q_refs]
        q = jnp.concatenate(qs, axis=1) if n_q > 1 else qs[0]
        s = lax.dot_general(q, k, (((1,), (1,)), ((), ())), preferred_element_type=F32)
        m = jnp.max(s, axis=1, keepdims=True)
        p = jnp.exp2((s - m) * exp2_scale)
        l = jnp.sum(p, axis=1, keepdims=True)
        o = jnp.dot(p.astype(BF16), v, preferred_element_type=F32)
        o_ref[r0:r0 + sub, :] = (o / l).astype(o_ref.dtype)


def attention(q_parts, k_parts, v_part, *, batch, heads, tq_total, tk_total, q_row0, k_row0,
              dv, scale, out_dtype):
    tq = _tile(tq_total, ATTN_TQ)
    nq = tq_total // tq
    assert q_row0 % tq == 0 and k_row0 % tk_total == 0
    qb0, kb0 = q_row0 // tq, k_row0 // tk_total

    def q_spec(fn):
        return pl.BlockSpec((tq, LANE), lambda b, h, i: (qb0 + b * nq + i, fn(h)))

    def k_spec(fn, width):
        return pl.BlockSpec((tk_total, width), lambda b, h, i: (kb0 + b, fn(h)))

    in_specs = ([q_spec(fn) for _, fn in q_parts] + [k_spec(fn, LANE) for _, fn in k_parts]
                + [k_spec(v_part[1], dv)])
    n_k = len(k_parts)
    kern = functools.partial(_attn_kernel, n_q=len(q_parts), n_k=n_k, exp2_scale=scale * math.log2(math.e))
    scratch = [pltpu.VMEM((tk_total, n_k * LANE), BF16)] if n_k > 1 else []
    return pl.pallas_call(
        kern,
        grid=(batch, heads, nq),
        in_specs=in_specs,
        out_specs=pl.BlockSpec((tq, dv), lambda b, h, i: (b * nq + i, h)),
        out_shape=jax.ShapeDtypeStruct((batch * tq_total, heads * dv), out_dtype),
        scratch_shapes=scratch,
        compiler_params=_params(("parallel", "parallel", "arbitrary")),
        name="attention",
    )(*[a for a, _ in q_parts], *[a for a, _ in k_parts], v_part[0])


MOE_TM = 512
MOE_GATHER_TM = 256
MOE_COMBINE_TM = 128
MOE_DOWN_TN = 2048
MOE_GU_TN = 512


def _moe_gather_kernel(tok_ref, tok_next_ref, nused_ref, h_ref, o_ref, buf, sem, *, tm, blocks_per_moe_block):
    b = pl.program_id(0)
    n_used = nused_ref[0] * blocks_per_moe_block
    slot = b % 2

    def start_rows(toks, dst_slot):
        def issue(r, carry):
            pltpu.make_async_copy(h_ref.at[pl.ds(toks[r], 1), :], buf.at[dst_slot, pl.ds(r, 1), :],
                                  sem.at[dst_slot]).start()
            return carry
        lax.fori_loop(0, tm, issue, 0)

    @pl.when(jnp.logical_and(b == 0, n_used > 0))
    def _():
        start_rows(tok_ref, 0)

    @pl.when(b + 1 < n_used)
    def _():
        start_rows(tok_next_ref, 1 - slot)

    @pl.when(b < n_used)
    def _():
        pltpu.make_async_copy(h_ref.at[pl.ds(0, tm), :], buf.at[slot], sem.at[slot]).wait()
        o_ref[...] = buf[slot].astype(o_ref.dtype)

    @pl.when(b >= n_used)
    def _():
        o_ref[...] = jnp.zeros(o_ref.shape, o_ref.dtype)


def moe_gather(h, slot_tok, nused, n_slots):
    n, d = h.shape
    tm = MOE_GATHER_TM
    nb = n_slots // tm
    kern = functools.partial(_moe_gather_kernel, tm=tm, blocks_per_moe_block=MOE_TM // tm)
    tok3 = slot_tok.reshape(nb, 1, tm)
    return pl.pallas_call(
        kern,
        grid=(nb,),
        in_specs=[pl.BlockSpec((None, None, tm), lambda b: (b, 0, 0), memory_space=pltpu.SMEM),
                  pl.BlockSpec((None, None, tm), lambda b: (jnp.minimum(b + 1, nb - 1), 0, 0),
                               memory_space=pltpu.SMEM),
                  pl.BlockSpec(memory_space=pltpu.SMEM),
                  pl.BlockSpec(memory_space=pl.ANY)],
        out_specs=pl.BlockSpec((tm, d), lambda b: (b, 0)),
        out_shape=jax.ShapeDtypeStruct((n_slots, d), BF16),
        scratch_shapes=[pltpu.VMEM((2, tm, d), F32), pltpu.SemaphoreType.DMA((2,))],
        compiler_params=_params(("arbitrary",)),
        name="moe_gather",
    )(tok3, tok3, nused, h)


def _expert_changed(be_ref, b):
    return jnp.logical_or(b == 0, be_ref[b] != be_ref[jnp.maximum(b - 1, 0)])


def _moe_gu_kernel(be_ref, nused_ref, x_ref, wg_ref, wu_ref, bg_ref, bu_ref, o_ref, wg_bf, wu_bf):
    b = pl.program_id(1)

    @pl.when(_expert_changed(be_ref, b))
    def _():
        wg_bf[...] = wg_ref[...].astype(BF16)
        wu_bf[...] = wu_ref[...].astype(BF16)

    @pl.when(b < nused_ref[0])
    def _():
        x = x_ref[...]
        g = jnp.dot(x, wg_bf[...], preferred_element_type=F32) + bg_ref[...]
        u = jnp.dot(x, wu_bf[...], preferred_element_type=F32) + bu_ref[...]
        gate_h = jnp.minimum(g, SWIGLU_LIMIT)
        up_h = jnp.clip(u, -SWIGLU_LIMIT, SWIGLU_LIMIT)
        o_ref[...] = ((up_h + 1.0) * gate_h * jax.nn.sigmoid(SWIGLU_ALPHA * gate_h)).astype(o_ref.dtype)

    @pl.when(b >= nused_ref[0])
    def _():
        o_ref[...] = jnp.zeros(o_ref.shape, o_ref.dtype)


def moe_gate_up(xs, w_gu, b_gu, layer, block_expert, nused):
    n_slots, d = xs.shape
    depth, n_e, _, two_ff = w_gu.shape
    d_ff = two_ff // 2
    tm = MOE_TM
    tn = _tile(d_ff, MOE_GU_TN)
    nj = d_ff // tn
    nb = n_slots // tm
    grid_spec = pltpu.PrefetchScalarGridSpec(
        num_scalar_prefetch=2,
        grid=(nj, nb),
        in_specs=[pl.BlockSpec((tm, d), lambda j, b, be, nu: (b, 0)),
                  pl.BlockSpec((None, None, d, tn), lambda j, b, be, nu: (layer, be[b], 0, j)),
                  pl.BlockSpec((None, None, d, tn), lambda j, b, be, nu: (layer, be[b], 0, nj + j)),
                  pl.BlockSpec((None, None, 1, tn), lambda j, b, be, nu: (layer, be[b], 0, j)),
                  pl.BlockSpec((None, None, 1, tn), lambda j, b, be, nu: (layer, be[b], 0, nj + j))],
        out_specs=pl.BlockSpec((tm, tn), lambda j, b, be, nu: (b, j)),
        scratch_shapes=[pltpu.VMEM((d, tn), BF16), pltpu.VMEM((d, tn), BF16)],
    )
    b3 = b_gu.reshape(depth, n_e, 1, two_ff)
    return pl.pallas_call(
        _moe_gu_kernel,
        grid_spec=grid_spec,
        out_shape=jax.ShapeDtypeStruct((n_slots, d_ff), BF16),
        compiler_params=_params(("arbitrary", "arbitrary")),
        name="moe_gate_up",
    )(block_expert, nused, xs, w_gu, w_gu, b3, b3)


def _moe_down_kernel(be_ref, nused_ref, h_ref, w_ref, bias_ref, o_ref, w_bf):
    b = pl.program_id(1)

    @pl.when(_expert_changed(be_ref, b))
    def _():
        w_bf[...] = w_ref[...].astype(BF16)

    @pl.when(b < nused_ref[0])
    def _():
        o_ref[...] = jnp.dot(h_ref[...], w_bf[...], preferred_element_type=F32) + bias_ref[...]

    @pl.when(b >= nused_ref[0])
    def _():
        o_ref[...] = jnp.zeros(o_ref.shape, o_ref.dtype)


def moe_down(hdn, w_down, b_down, layer, block_expert, nused):
    n_slots, d_ff = hdn.shape
    depth, n_e, _, d = w_down.shape
    tm = MOE_TM
    tn = _tile(d, MOE_DOWN_TN)
    nb = n_slots // tm
    grid_spec = pltpu.PrefetchScalarGridSpec(
        num_scalar_prefetch=2,
        grid=(d // tn, nb),
        in_specs=[pl.BlockSpec((tm, d_ff), lambda j, b, be, nu: (b, 0)),
                  pl.BlockSpec((None, None, d_ff, tn), lambda j, b, be, nu: (layer, be[b], 0, j)),
                  pl.BlockSpec((None, None, 1, tn), lambda j, b, be, nu: (layer, be[b], 0, j))],
        out_specs=pl.BlockSpec((tm, tn), lambda j, b, be, nu: (b, j)),
        scratch_shapes=[pltpu.VMEM((d_ff, tn), BF16)],
    )
    return pl.pallas_call(
        _moe_down_kernel,
        grid_spec=grid_spec,
        out_shape=jax.ShapeDtypeStruct((n_slots, d), F32),
        compiler_params=_params(("arbitrary", "arbitrary")),
        name="moe_down",
    )(block_expert, nused, hdn, w_down, b_down.reshape(depth, n_e, 1, d))


def _moe_combine_kernel(pos_ref, pos_next_ref, gates_ref, x_ref, y_ref, gain_ref, g_ref, o_ref, buf, sem,
                        *, tm, nb):
    i = pl.program_id(0)
    slot = i % 2

    def start_rows(pos, dst_slot):
        def issue(r, carry):
            for k in range(TOP_K):
                pltpu.make_async_copy(y_ref.at[pl.ds(pos[r * TOP_K + k], 1), :],
                                      buf.at[dst_slot, k, pl.ds(r, 1), :], sem.at[dst_slot]).start()
            return carry
        lax.fori_loop(0, tm, issue, 0)

    @pl.when(i == 0)
    def _():
        start_rows(pos_ref, 0)

    @pl.when(i + 1 < nb)
    def _():
        start_rows(pos_next_ref, 1 - slot)

    for k in range(TOP_K):
        pltpu.make_async_copy(y_ref.at[pl.ds(0, tm), :], buf.at[slot, k], sem.at[slot]).wait()
    gates = gates_ref[...]
    f = buf[slot, 0] * gates[:, 0:1]
    for k in range(1, TOP_K):
        f = f + buf[slot, k] * gates[:, k:k + 1]
    y = f * lax.rsqrt(jnp.mean(f * f, axis=-1, keepdims=True) + RMS_EPS) * gain_ref[...]
    o_ref[...] = x_ref[...] + g_ref[...] * y


def moe_combine_resid(x, y, pos, gates, gain, gate_mod, n_prompt, rows_per_sample):
    n, d = x.shape
    tm = _tile(math.gcd(n_prompt, rows_per_sample), MOE_COMBINE_TM)
    seg = functools.partial(_seg_of_block, tm=tm, n_prompt=n_prompt, rows_per_sample=rows_per_sample)
    s = gate_mod.shape[0]
    nb = n // tm
    kern = functools.partial(_moe_combine_kernel, tm=tm, nb=nb)
    pos3 = pos.reshape(nb, 1, tm * TOP_K)
    return pl.pallas_call(
        kern,
        grid=(nb,),
        in_specs=[pl.BlockSpec((None, None, tm * TOP_K), lambda i: (i, 0, 0), memory_space=pltpu.SMEM),
                  pl.BlockSpec((None, None, tm * TOP_K), lambda i: (jnp.minimum(i + 1, nb - 1), 0, 0),
                               memory_space=pltpu.SMEM),
                  pl.BlockSpec((tm, TOP_K), lambda i: (i, 0)),
                  pl.BlockSpec((tm, d), lambda i: (i, 0)),
                  pl.BlockSpec(memory_space=pl.ANY),
                  pl.BlockSpec((1, d), lambda i: (0, 0)),
                  pl.BlockSpec((None, 1, d), lambda i: (seg(i), 0, 0))],
        out_specs=pl.BlockSpec((tm, d), lambda i: (i, 0)),
        out_shape=jax.ShapeDtypeStruct((n, d), F32),
        scratch_shapes=[pltpu.VMEM((2, TOP_K, tm, d), F32), pltpu.SemaphoreType.DMA((2,))],
        compiler_params=_params(("arbitrary",)),
        name="moe_combine",
    )(pos3, pos3, gates, x, y, gain.reshape(1, d), gate_mod.reshape(s, 1, d))


def moe_route(h, w_r, b_r):
    n, d = h.shape
    w_pad = jnp.pad(w_r, ((0, 0), (0, LANE - N_EXPERTS))).astype(BF16)
    logits = mm(h, w_pad, tm=512, tn=LANE)[:, :N_EXPERTS] + b_r
    top_logit, top_idx = lax.top_k(logits, TOP_K)
    gates = jax.nn.softmax(top_logit, axis=-1)
    a = n * TOP_K
    flat_e = top_idx.reshape(a)
    onehot = (flat_e[:, None] == jnp.arange(N_EXPERTS, dtype=flat_e.dtype)[None, :]).astype(jnp.int32)
    csum = jnp.cumsum(onehot, axis=0)
    rank = jnp.take_along_axis(csum, flat_e[:, None], axis=1)[:, 0] - 1
    counts = csum[-1]
    padded = (counts + MOE_TM - 1) // MOE_TM * MOE_TM
    pad_end = jnp.cumsum(padded)
    pad_start = pad_end - padded
    dest = (pad_start[flat_e] + rank).astype(jnp.int32)
    n_blocks = a // MOE_TM + N_EXPERTS
    n_slots = n_blocks * MOE_TM
    slot_tok = jnp.zeros((n_slots,), jnp.int32).at[dest].set(jnp.arange(a, dtype=jnp.int32) // TOP_K)
    block_expert = jnp.minimum(
        jnp.searchsorted(pad_end, jnp.arange(n_blocks, dtype=jnp.int32) * MOE_TM, side='right'),
        N_EXPERTS - 1).astype(jnp.int32)
    nused = (pad_end[-1] // MOE_TM).astype(jnp.int32).reshape(1)
    return slot_tok, block_expert, nused, dest.reshape(n, TOP_K), gates, n_slots


def moe_block(x, h, w_r, b_r, w_gu, b_gu, w_down, b_down, layer, gain, gate_mod, n_prompt, rows_per_sample):
    slot_tok, block_expert, nused, pos, gates, n_slots = moe_route(h, w_r, b_r)
    xs = moe_gather(h, slot_tok, nused, n_slots)
    hdn = moe_gate_up(xs, w_gu, b_gu, layer, block_expert, nused)
    y = moe_down(hdn, w_down, b_down, layer, block_expert, nused)
    return moe_combine_resid(x, y, pos, gates, gain, gate_mod, n_prompt, rows_per_sample)


def _rms(x, gain):
    return x * lax.rsqrt(jnp.mean(x * x, axis=-1, keepdims=True) + RMS_EPS) * gain


def _rope_tables(n_tokens, rot_dim):
    rows = n_tokens // GRID_W
    row = jnp.repeat(jnp.arange(rows, dtype=F32), GRID_W)
    col = jnp.tile(jnp.arange(GRID_W, dtype=F32), rows)
    axis_dim = rot_dim // 2
    inv_freq = ROPE_THETA ** (-jnp.arange(0, axis_dim, 2, dtype=F32) / axis_dim)
    ang = jnp.concatenate([row[:, None] * inv_freq, col[:, None] * inv_freq], axis=-1)
    cos, sin = jnp.cos(ang), jnp.sin(ang)
    return jnp.concatenate([cos, cos], axis=-1), jnp.concatenate([-sin, sin], axis=-1)


def _rope(x, c_tab, s_tab):
    shape = (c_tab.shape[0],) + (1,) * (x.ndim - 3) + (c_tab.shape[1],)
    return x * c_tab.reshape(shape) + jnp.roll(x, x.shape[-1] // 2, axis=-1) * s_tab.reshape(shape)


def _rope_sample_rows(x, tabs, n_p, bs, ts):
    xs = x[n_p:].reshape((bs, ts) + x.shape[1:])
    return jnp.concatenate([x[:n_p], _rope(xs, *tabs).reshape((bs * ts,) + x.shape[1:])], axis=0)


def _keys(bs, cache, own_sample, own_prompt):
    w = own_prompt.shape[-1]
    s = jnp.concatenate([cache.reshape(bs, -1, w), own_sample.reshape(bs, -1, w)], axis=1)
    return jnp.concatenate([s.reshape(-1, w), own_prompt], axis=0)


def _attend_groups(make_parts, *, dims, heads, dv, scale, out_dtype):
    bp, tp, bs, ts, past = dims
    q_parts, k_parts, v_part = make_parts
    o_p = attention(q_parts, k_parts, v_part, batch=bp, heads=heads, tq_total=tp, tk_total=tp,
                    q_row0=0, k_row0=bs * (past + ts), dv=dv, scale=scale, out_dtype=out_dtype)
    o_s = attention(q_parts, k_parts, v_part, batch=bs, heads=heads, tq_total=ts, tk_total=past + ts,
                    q_row0=bp * tp, k_row0=0, dv=dv, scale=scale, out_dtype=out_dtype)
    return jnp.concatenate([o_p, o_s], axis=0)


def even_mixer(h, dims, ropes, caches, w_in, q_norm, w_uq, kv_norm, w_ukv, lam_params, subln, w_out,
               lambda_init):
    bp, tp, bs, ts, past = dims
    n_p = bp * tp
    n = h.shape[0]
    c1 = MLA_Q_RANK + MLA_KV_RANK
    c2 = c1 + MLA_ROPE_DIM
    c3 = c2 + 2 * DIFF_QK_WIDTH
    p128, p64 = _deinterleave_matrix(LANE), _deinterleave_matrix(MLA_ROPE_DIM)
    w_in_bf = w_in.astype(BF16)
    a = mm(h, w_in_bf[:, :c1])
    w_kr = jnp.pad(w_in[:, c1:c2], ((0, 0), (0, LANE - MLA_ROPE_DIM)))
    kr = mm(h, permute_lanes(w_kr, p64), tn=LANE)[:, :MLA_ROPE_DIM]
    kr_own = mm(h, w_kr.astype(BF16), tn=LANE, rows=n_p)[:, :MLA_ROPE_DIM]
    df = mm(h, jnp.concatenate([permute_lanes(w_in[:, c2:c3], p128), w_in_bf[:, c3:]], axis=1))
    dk_own = mm(h, w_in_bf[:, c2 + DIFF_QK_WIDTH:c3], rows=n_p)
    cq, ckv = a[:, :MLA_Q_RANK], a[:, MLA_Q_RANK:]
    w_uq3 = w_uq.reshape(MLA_Q_RANK, MLA_HEADS, MLA_NOPE_DIM + MLA_ROPE_DIM)
    w_uq_perm = jnp.concatenate(
        [w_uq3[:, :, :MLA_NOPE_DIM].reshape(MLA_Q_RANK, -1).astype(BF16),
         permute_lanes(w_uq3[:, :, MLA_NOPE_DIM:].reshape(MLA_Q_RANK, -1), p64)], axis=1)
    q = mm(_rms(cq, q_norm).astype(BF16), w_uq_perm)
    n_nope = MLA_HEADS * MLA_NOPE_DIM
    q_nope = q[:, :n_nope].astype(BF16)
    q_rope = q[:, n_nope:].reshape(n, MLA_HEADS, MLA_ROPE_DIM)
    ckv = _rms(ckv, kv_norm)
    dq = df[:, :DIFF_QK_WIDTH].reshape(n, DIFF_HEADS, 2, DIFF_HEAD_DIM)
    dk = df[:, DIFF_QK_WIDTH:2 * DIFF_QK_WIDTH].reshape(n, DIFF_HEADS, 2, DIFF_HEAD_DIM)
    dv = df[:, 2 * DIFF_QK_WIDTH:]

    q_rope = _rope_sample_rows(q_rope, ropes[0], n_p, bs, ts)
    kr_r = _rope_sample_rows(kr, ropes[0], n_p, bs, ts).astype(BF16)
    dq = _rope_sample_rows(dq, ropes[1], n_p, bs, ts)
    dk_r = _rope_sample_rows(dk, ropes[1], n_p, bs, ts).reshape(n, -1).astype(BF16)

    own = (ckv[:n_p].reshape(bp, tp, MLA_KV_RANK), kr_own.reshape(bp, tp, MLA_ROPE_DIM),
           dk_own.reshape(bp, tp, DIFF_HEADS, 2 * DIFF_HEAD_DIM),
           dv[:n_p].reshape(bp, tp, DIFF_HEADS, DIFF_V_DIM))

    c_ckv, c_kr, c_dk, c_dv = caches
    ckv_bf = ckv.astype(BF16)
    ckv_keys = _keys(bs, c_ckv.astype(BF16), ckv_bf[n_p:], ckv_bf[:n_p])
    kv = mm(ckv_keys, w_ukv.astype(BF16), out_dtype=BF16)
    c_kr_perm = c_kr[..., _deinterleave_src(MLA_ROPE_DIM)[:MLA_ROPE_DIM]].astype(BF16)
    kr_keys = jnp.pad(_keys(bs, c_kr_perm, kr_r[n_p:], kr_r[:n_p]), ((0, 0), (0, LANE - MLA_ROPE_DIM)))
    q_rope_pad = jnp.pad(q_rope, ((0, 0), (0, 0), (0, LANE - MLA_ROPE_DIM))).reshape(n, -1).astype(BF16)
    a_out = _attend_groups(
        ([(q_nope, lambda hh: hh), (q_rope_pad, lambda hh: hh)],
         [(kv, lambda hh: 2 * hh), (kr_keys, lambda hh: 0)],
         (kv, lambda hh: 2 * hh + 1)),
        dims=dims, heads=MLA_HEADS, dv=MLA_V_DIM,
        scale=(MLA_NOPE_DIM + MLA_ROPE_DIM) ** -0.5, out_dtype=BF16)

    c_dk_perm = permute_lanes(c_dk.reshape(bs * past, -1), p128)
    dk_keys = _keys(bs, c_dk_perm, dk_r[n_p:], dk_r[:n_p])
    dv_bf = dv.astype(BF16)
    dv_keys = _keys(bs, c_dv.astype(BF16), dv_bf[n_p:], dv_bf[:n_p])
    o2 = _attend_groups(
        ([(dq.reshape(n, -1).astype(BF16), lambda hh: hh)],
         [(dk_keys, lambda hh: hh)],
         (dv_keys, lambda hh: hh // 2)),
        dims=dims, heads=2 * DIFF_HEADS, dv=DIFF_V_DIM,
        scale=DIFF_HEAD_DIM ** -0.5, out_dtype=F32).reshape(n, DIFF_HEADS, 2, DIFF_V_DIM)
    lq1, lk1, lq2, lk2 = lam_params.astype(F32)
    lam = jnp.exp(jnp.sum(lq1 * lk1)) - jnp.exp(jnp.sum(lq2 * lk2)) + lambda_init
    b_out = o2[:, :, 0] - lam * o2[:, :, 1]
    b_out = _rms(b_out, subln) * (1.0 - lambda_init)
    merged = jnp.concatenate([a_out, b_out.reshape(n, -1).astype(BF16)], axis=-1)
    return mm(merged, w_out.astype(BF16)), own


def odd_mixer(h, dims, ropes, caches, w_in, q_norm, k_norm, w_out):
    bp, tp, bs, ts, past = dims
    n_p = bp * tp
    n = h.shape[0]
    nq = GQA_Q_HEADS * GQA_HEAD_DIM
    nk = GQA_KV_HEADS * GQA_HEAD_DIM
    p128 = _deinterleave_matrix(LANE)
    src = _deinterleave_src(LANE)
    w_in_bf = w_in.astype(BF16)
    qkv = mm(h, jnp.concatenate([permute_lanes(w_in[:, :nq + nk], p128), w_in_bf[:, nq + nk:]], axis=1))
    k_own = _rms(mm(h, w_in_bf[:, nq:nq + nk], rows=n_p).reshape(n_p, GQA_KV_HEADS, GQA_HEAD_DIM), k_norm)
    q = _rms(qkv[:, :nq].reshape(n, GQA_Q_HEADS, GQA_HEAD_DIM), q_norm[src])
    k = _rms(qkv[:, nq:nq + nk].reshape(n, GQA_KV_HEADS, GQA_HEAD_DIM), k_norm[src])
    v = qkv[:, nq + nk:]
    q_r = _rope_sample_rows(q, ropes[2], n_p, bs, ts).reshape(n, -1).astype(BF16)
    k_r = _rope_sample_rows(k, ropes[2], n_p, bs, ts).reshape(n, -1).astype(BF16)
    own = (k_own.reshape(bp, tp, GQA_KV_HEADS, GQA_HEAD_DIM),
           v[:n_p].reshape(bp, tp, GQA_KV_HEADS, GQA_HEAD_DIM))
    c_k, c_v = caches
    k_keys = _keys(bs, permute_lanes(c_k.reshape(bs * past, -1), p128), k_r[n_p:], k_r[:n_p])
    v_bf = v.astype(BF16)
    v_keys = _keys(bs, c_v.astype(BF16), v_bf[n_p:], v_bf[:n_p])
    rep = GQA_Q_HEADS // GQA_KV_HEADS
    o = _attend_groups(
        ([(q_r, lambda hh: hh)],
         [(k_keys, lambda hh: hh // rep)],
         (v_keys, lambda hh: hh // rep)),
        dims=dims, heads=GQA_Q_HEADS, dv=GQA_HEAD_DIM, scale=GQA_HEAD_DIM ** -0.5, out_dtype=BF16)
    return mm(o, w_out.astype(BF16)), own


def kernel(x_prompt, x_sample, cache_mla_ckv, cache_mla_krope, cache_diff_k, cache_diff_v, cache_gqa_k, cache_gqa_v, c, c_ctx, mod_w, mod_b, norm_gains, even_w_in, mla_q_norm, mla_w_uq, mla_kv_norm, mla_w_ukv, diff_lambda, diff_subln, even_w_out, odd_w_in, gqa_q_norm, gqa_k_norm, odd_w_out, router_w, router_b, moe_w_gu, moe_b_gu, moe_w_down, moe_b_down):
    bp, tp, d = x_prompt.shape
    bs, ts, _ = x_sample.shape
    past = cache_mla_ckv.shape[2]
    depth = mod_w.shape[0]
    n_p = bp * tp
    dims = (bp, tp, bs, ts, past)
    x = jnp.concatenate([x_prompt.reshape(n_p, d), x_sample.reshape(bs * ts, d)], axis=0)
    cond = jnp.concatenate([c_ctx[None, :], c], axis=0)
    n_seg = cond.shape[0]
    cond_act = jnp.pad(jax.nn.silu(cond), ((0, (-n_seg) % 16), (0, 0))).astype(BF16)
    ropes = (_rope_tables(ts, MLA_ROPE_DIM), _rope_tables(ts, DIFF_HEAD_DIM), _rope_tables(ts, GQA_HEAD_DIM))

    even_states, odd_states = [], []
    for layer in range(depth):
        mod = mm(cond_act, mod_w, w_index=layer)[:n_seg] + mod_b[layer]
        sh_m, sc_m, g_m, sh_f, sc_f, g_f = jnp.split(mod, 6, axis=-1)
        gains = norm_gains[layer]
        h = norm_mod(x, gains[0], sc_m, sh_m, n_p, ts, BF16)
        i = layer // 2
        if layer % 2 == 0:
            caches = (cache_mla_ckv[:, i], cache_mla_krope[:, i], cache_diff_k[:, i], cache_diff_v[:, i])
            lambda_init = 0.8 - 0.6 * math.exp(-0.3 * layer)
            m, own = even_mixer(h, dims, ropes, caches, even_w_in[i], mla_q_norm[i], mla_w_uq[i],
                                mla_kv_norm[i], mla_w_ukv[i], diff_lambda[i], diff_subln[i], even_w_out[i],
                                lambda_init)
            even_states.append(own)
        else:
            caches = (cache_gqa_k[:, i], cache_gqa_v[:, i])
            m, own = odd_mixer(h, dims, ropes, caches, odd_w_in[i], gqa_q_norm[i], gqa_k_norm[i], odd_w_out[i])
            odd_states.append(own)
        x = resid(x, m, gains[1], g_m, n_p, ts)
        h = norm_mod(x, gains[2], sc_f, sh_f, n_p, ts, F32)
        x = moe_block(x, h, router_w[layer], router_b[layer], moe_w_gu, moe_b_gu, moe_w_down, moe_b_down,
                      layer, gains[3], g_f, n_p, ts)

    y_prompt = x[:n_p].reshape(bp, tp, d)
    y_sample = x[n_p:].reshape(bs, ts, d)
    return (y_prompt, y_sample,
            jnp.stack([s[0] for s in even_states], axis=1),
            jnp.stack([s[1] for s in even_states], axis=1),
            jnp.stack([s[2] for s in even_states], axis=1),
            jnp.stack([s[3] for s in even_states], axis=1),
            jnp.stack([s[0] for s in odd_states], axis=1),
            jnp.stack([s[1] for s in odd_states], axis=1))
```

```python
import functools
import math

import jax
import jax.numpy as jnp
import numpy as np
from jax import lax
from jax.experimental import pallas as pl
from jax.experimental.pallas import tpu as pltpu

F32 = jnp.float32
BF16 = jnp.bfloat16

GRID_W = 64
ROPE_THETA = 10000.0
RMS_EPS = 1e-6
MLA_HEADS = 16
MLA_Q_RANK = 1024
MLA_KV_RANK = 512
MLA_NOPE_DIM = 128
MLA_ROPE_DIM = 64
MLA_V_DIM = 128
DIFF_HEADS = 8
DIFF_HEAD_DIM = 128
DIFF_V_DIM = 2 * DIFF_HEAD_DIM
DIFF_QK_WIDTH = DIFF_HEADS * 2 * DIFF_HEAD_DIM
GQA_Q_HEADS = 32
GQA_KV_HEADS = 8
GQA_HEAD_DIM = 128
N_EXPERTS = 32
TOP_K = 4
SWIGLU_LIMIT = 7.0
SWIGLU_ALPHA = 1.702

LANE = 128
ATTN_TQ = 2048
ATTN_SUB = 256
VMEM_LIMIT = 56 * 1024 * 1024


def _tile(dim, pref):
    t = min(pref, dim)
    while dim % t:
        t //= 2
    return t


def _params(sem):
    return pltpu.CompilerParams(dimension_semantics=sem, vmem_limit_bytes=VMEM_LIMIT)


def _mm_kernel(x_ref, w_ref, o_ref):
    o_ref[...] = jnp.dot(x_ref[...].astype(BF16), w_ref[...].astype(BF16),
                         preferred_element_type=F32).astype(o_ref.dtype)


def mm(x, w, out_dtype=F32, tm=1024, tn=512, w_index=None, rows=None):
    m, k = x.shape
    m = m if rows is None else rows
    n = w.shape[-1]
    tm = _tile(m, tm)
    tn = _tile(n, tn)
    if w_index is None:
        w_spec = pl.BlockSpec((k, tn), lambda i, j: (0, j))
    else:
        w_spec = pl.BlockSpec((None, k, tn), lambda i, j: (w_index, 0, j))
    return pl.pallas_call(
        _mm_kernel,
        grid=(m // tm, n // tn),
        in_specs=[pl.BlockSpec((tm, k), lambda i, j: (i, 0)), w_spec],
        out_specs=pl.BlockSpec((tm, tn), lambda i, j: (i, j)),
        out_shape=jax.ShapeDtypeStruct((m, n), out_dtype),
        compiler_params=_params(("parallel", "parallel")),
        name="mm",
    )(x, w)


def _deinterleave_src(blk):
    j = np.arange(LANE)
    o = j % blk
    return (j // blk) * blk + np.where(o < blk // 2, 2 * o, 2 * (o - blk // 2) + 1)


def _deinterleave_matrix(blk):
    p = np.zeros((LANE, LANE), np.float32)
    p[_deinterleave_src(blk), np.arange(LANE)] = 1.0
    return jnp.asarray(p, BF16)


def _permute_kernel(w_ref, p_ref, o_ref):
    o_ref[...] = jnp.dot(w_ref[...].astype(BF16), p_ref[...], preferred_element_type=F32).astype(o_ref.dtype)


def permute_lanes(w, perm):
    r, c = w.shape
    tm = _tile(r, 1024)
    return pl.pallas_call(
        _permute_kernel,
        grid=(r // tm, c // LANE),
        in_specs=[pl.BlockSpec((tm, LANE), lambda i, j: (i, j)),
                  pl.BlockSpec((LANE, LANE), lambda i, j: (0, 0))],
        out_specs=pl.BlockSpec((tm, LANE), lambda i, j: (i, j)),
        out_shape=jax.ShapeDtypeStruct((r, c), BF16),
        compiler_params=_params(("parallel", "parallel")),
        name="permute_lanes",
    )(w, perm)


def _seg_of_block(i, tm, n_prompt, rows_per_sample):
    pb = n_prompt // tm
    return jnp.where(i < pb, 0, 1 + (i - pb) // (rows_per_sample // tm))


def _norm_mod_kernel(x_ref, gain_ref, sc_ref, sh_ref, o_ref):
    x = x_ref[...]
    y = x * lax.rsqrt(jnp.mean(x * x, axis=-1, keepdims=True) + RMS_EPS) * gain_ref[...]
    o_ref[...] = (y * (1.0 + sc_ref[...]) + sh_ref[...]).astype(o_ref.dtype)


def norm_mod(x, gain, scale, shift, n_prompt, rows_per_sample, out_dtype):
    n, d = x.shape
    tm = _tile(math.gcd(n_prompt, rows_per_sample), 256)
    seg = functools.partial(_seg_of_block, tm=tm, n_prompt=n_prompt, rows_per_sample=rows_per_sample)
    s = scale.shape[0]
    return pl.pallas_call(
        _norm_mod_kernel,
        grid=(n // tm,),
        in_specs=[pl.BlockSpec((tm, d), lambda i: (i, 0)),
                  pl.BlockSpec((1, d), lambda i: (0, 0)),
                  pl.BlockSpec((None, 1, d), lambda i: (seg(i), 0, 0)),
                  pl.BlockSpec((None, 1, d), lambda i: (seg(i), 0, 0))],
        out_specs=pl.BlockSpec((tm, d), lambda i: (i, 0)),
        out_shape=jax.ShapeDtypeStruct((n, d), out_dtype),
        compiler_params=_params(("parallel",)),
        name="norm_mod",
    )(x, gain.reshape(1, d), scale.reshape(s, 1, d), shift.reshape(s, 1, d))


def _resid_norm_mod_kernel(x_ref, m_ref, gain_ref, g_ref, gain2_ref, sc_ref, sh_ref, xo_ref, ho_ref):
    m = m_ref[...]
    y = m * lax.rsqrt(jnp.mean(m * m, axis=-1, keepdims=True) + RMS_EPS) * gain_ref[...]
    x = x_ref[...] + g_ref[...] * y
    xo_ref[...] = x
    z = x * lax.rsqrt(jnp.mean(x * x, axis=-1, keepdims=True) + RMS_EPS) * gain2_ref[...]
    ho_ref[...] = (z * (1.0 + sc_ref[...]) + sh_ref[...]).astype(ho_ref.dtype)


def resid_norm_mod(x, m, gain, gate, gain2, scale, shift, n_prompt, rows_per_sample):
    n, d = x.shape
    tm = _tile(math.gcd(n_prompt, rows_per_sample), 256)
    seg = functools.partial(_seg_of_block, tm=tm, n_prompt=n_prompt, rows_per_sample=rows_per_sample)
    s = gate.shape[0]
    row = pl.BlockSpec((tm, d), lambda i: (i, 0))
    vec = pl.BlockSpec((1, d), lambda i: (0, 0))
    seg_vec = pl.BlockSpec((None, 1, d), lambda i: (seg(i), 0, 0))
    return pl.pallas_call(
        _resid_norm_mod_kernel,
        grid=(n // tm,),
        in_specs=[row, row, vec, seg_vec, vec, seg_vec, seg_vec],
        out_specs=[row, row],
        out_shape=[jax.ShapeDtypeStruct((n, d), F32), jax.ShapeDtypeStruct((n, d), F32)],
        compiler_params=_params(("parallel",)),
        name="resid_norm_mod",
    )(x, m, gain.reshape(1, d), gate.reshape(s, 1, d), gain2.reshape(1, d),
      scale.reshape(s, 1, d), shift.reshape(s, 1, d))


def _rope_cast_kernel(*refs, has_gain):
    x_ref, c_ref, s_ref = refs[:3]
    o_ref = refs[-1]
    c = c_ref[...]
    s = s_ref[...]
    for k in range(x_ref.shape[1] // LANE):
        a = x_ref[:, k * LANE:(k + 1) * LANE]
        if has_gain:
            a = a * lax.rsqrt(jnp.mean(a * a, axis=-1, keepdims=True) + RMS_EPS) * refs[3][...]
        o_ref[:, k * LANE:(k + 1) * LANE] = (a * c + pltpu.roll(a, LANE // 2, axis=1) * s).astype(o_ref.dtype)


def rope_cast(x, col0, width, c_rows, s_rows, gain=None):
    n = x.shape[0]
    tm = _tile(n, 512)
    tw = _tile(width, 512)
    assert col0 % tw == 0
    cb = col0 // tw
    in_specs = [pl.BlockSpec((tm, tw), lambda i, j: (i, cb + j)),
                pl.BlockSpec((tm, LANE), lambda i, j: (i, 0)),
                pl.BlockSpec((tm, LANE), lambda i, j: (i, 0))]
    args = [x, c_rows, s_rows]
    if gain is not None:
        in_specs.append(pl.BlockSpec((1, LANE), lambda i, j: (0, 0)))
        args.append(gain.reshape(1, LANE))
    return pl.pallas_call(
        functools.partial(_rope_cast_kernel, has_gain=gain is not None),
        grid=(n // tm, width // tw),
        in_specs=in_specs,
        out_specs=pl.BlockSpec((tm, tw), lambda i, j: (i, j)),
        out_shape=jax.ShapeDtypeStruct((n, width), BF16),
        compiler_params=_params(("parallel", "parallel")),
        name="rope_cast",
    )(*args)


def _attn_kernel(*refs, n_q, n_k, exp2_scale):
    q_refs = refs[:n_q]
    k_refs = refs[n_q:n_q + n_k]
    v_ref = refs[n_q + n_k]
    o_ref = refs[n_q + n_k + 1]

    if n_k > 1:
        kcat = refs[n_q + n_k + 2]

        @pl.when(pl.program_id(2) == 0)
        def _():
            for idx, r in enumerate(k_refs):
                kcat[:, idx * LANE:(idx + 1) * LANE] = r[...]
        k = kcat[...]
    else:
        k = k_refs[0][...]
    v = v_ref[...]
    tq = o_ref.shape[0]
    sub = min(tq, ATTN_SUB)
    for r0 in range(0, tq, sub):
        qs = [r[r0:r0 + sub, :] for r in q_refs]
        q = jnp.concatenate(qs, axis=1) if n_q > 1 else qs[0]
        s = lax.dot_general(q, k, (((1,), (1,)), ((), ())), preferred_element_type=F32)
        m = jnp.max(s, axis=1, keepdims=True)
        p = jnp.exp2((s - m) * exp2_scale)
        l = jnp.sum(p, axis=1, keepdims=True)
        o = jnp.dot(p.astype(BF16), v, preferred_element_type=F32)
        o_ref[r0:r0 + sub, :] = (o / l).astype(o_ref.dtype)


def attention(q_parts, k_parts, v_part, *, batch, heads, tq_total, tk_total, q_row0, k_row0,
              dv, scale, out_dtype):
    tq = _tile(tq_total, ATTN_TQ)
    nq = tq_total // tq
    assert q_row0 % tq == 0 and k_row0 % tk_total == 0
    qb0, kb0 = q_row0 // tq, k_row0 // tk_total

    def q_spec(fn):
        return pl.BlockSpec((tq, LANE), lambda b, h, i: (qb0 + b * nq + i, fn(h)))

    def k_spec(fn, width):
        return pl.BlockSpec((tk_total, width), lambda b, h, i: (kb0 + b, fn(h)))

    in_specs = ([q_spec(fn) for _, fn in q_parts] + [k_spec(fn, LANE) for _, fn in k_parts]
                + [k_spec(v_part[1], dv)])
    n_k = len(k_parts)
    kern = functools.partial(_attn_kernel, n_q=len(q_parts), n_k=n_k, exp2_scale=scale * math.log2(math.e))
    scratch = [pltpu.VMEM((tk_total, n_k * LANE), BF16)] if n_k > 1 else []
    return pl.pallas_call(
        kern,
        grid=(batch, heads, nq),
        in_specs=in_specs,
        out_specs=pl.BlockSpec((tq, dv), lambda b, h, i: (b * nq + i, h)),
        out_shape=jax.ShapeDtypeStruct((batch * tq_total, heads * dv), out_dtype),
        scratch_shapes=scratch,
        compiler_params=_params(("parallel", "parallel", "arbitrary")),
        name="attention",
    )(*[a for a, _ in q_parts], *[a for a, _ in k_parts], v_part[0])


MOE_TM = 512
MOE_GATHER_TM = 256
MOE_COMBINE_TM = 128
MOE_DOWN_TN = 2048
MOE_GU_TN = 512


def _moe_gather_kernel(tok_ref, tok_next_ref, nused_ref, h_ref, o_ref, buf, sem, *, tm, blocks_per_moe_block):
    b = pl.program_id(0)
    n_used = nused_ref[0] * blocks_per_moe_block
    slot = b % 2

    def start_rows(toks, dst_slot):
        def issue(r, carry):
            pltpu.make_async_copy(h_ref.at[pl.ds(toks[r], 1), :], buf.at[dst_slot, pl.ds(r, 1), :],
                                  sem.at[dst_slot]).start()
            return carry
        lax.fori_loop(0, tm, issue, 0)

    @pl.when(jnp.logical_and(b == 0, n_used > 0))
    def _():
        start_rows(tok_ref, 0)

    @pl.when(b + 1 < n_used)
    def _():
        start_rows(tok_next_ref, 1 - slot)

    @pl.when(b < n_used)
    def _():
        pltpu.make_async_copy(h_ref.at[pl.ds(0, tm), :], buf.at[slot], sem.at[slot]).wait()
        o_ref[...] = buf[slot].astype(o_ref.dtype)

    @pl.when(b >= n_used)
    def _():
        o_ref[...] = jnp.zeros(o_ref.shape, o_ref.dtype)


def moe_gather(h, slot_tok, nused, n_slots):
    n, d = h.shape
    tm = MOE_GATHER_TM
    nb = n_slots // tm
    kern = functools.partial(_moe_gather_kernel, tm=tm, blocks_per_moe_block=MOE_TM // tm)
    tok3 = slot_tok.reshape(nb, 1, tm)
    return pl.pallas_call(
        kern,
        grid=(nb,),
        in_specs=[pl.BlockSpec((None, None, tm), lambda b: (b, 0, 0), memory_space=pltpu.SMEM),
                  pl.BlockSpec((None, None, tm), lambda b: (jnp.minimum(b + 1, nb - 1), 0, 0),
                               memory_space=pltpu.SMEM),
                  pl.BlockSpec(memory_space=pltpu.SMEM),
                  pl.BlockSpec(memory_space=pl.ANY)],
        out_specs=pl.BlockSpec((tm, d), lambda b: (b, 0)),
        out_shape=jax.ShapeDtypeStruct((n_slots, d), BF16),
        scratch_shapes=[pltpu.VMEM((2, tm, d), F32), pltpu.SemaphoreType.DMA((2,))],
        compiler_params=_params(("arbitrary",)),
        name="moe_gather",
    )(tok3, tok3, nused, h)


def _expert_changed(be_ref, b):
    return jnp.logical_or(b == 0, be_ref[b] != be_ref[jnp.maximum(b - 1, 0)])


def _moe_gu_kernel(be_ref, nused_ref, x_ref, wg_ref, wu_ref, bg_ref, bu_ref, o_ref, wg_bf, wu_bf):
    b = pl.program_id(1)

    @pl.when(_expert_changed(be_ref, b))
    def _():
        wg_bf[...] = wg_ref[...].astype(BF16)
        wu_bf[...] = wu_ref[...].astype(BF16)

    @pl.when(b < nused_ref[0])
    def _():
        x = x_ref[...]
        g = jnp.dot(x, wg_bf[...], preferred_element_type=F32) + bg_ref[...]
        u = jnp.dot(x, wu_bf[...], preferred_element_type=F32) + bu_ref[...]
        gate_h = jnp.minimum(g, SWIGLU_LIMIT)
        up_h = jnp.clip(u, -SWIGLU_LIMIT, SWIGLU_LIMIT)
        o_ref[...] = ((up_h + 1.0) * gate_h * jax.nn.sigmoid(SWIGLU_ALPHA * gate_h)).astype(o_ref.dtype)

    @pl.when(b >= nused_ref[0])
    def _():
        o_ref[...] = jnp.zeros(o_ref.shape, o_ref.dtype)


def moe_gate_up(xs, w_gu, b_gu, layer, block_expert, nused):
    n_slots, d = xs.shape
    depth, n_e, _, two_ff = w_gu.shape
    d_ff = two_ff // 2
    tm = MOE_TM
    tn = _tile(d_ff, MOE_GU_TN)
    nj = d_ff // tn
    nb = n_slots // tm
    grid_spec = pltpu.PrefetchScalarGridSpec(
        num_scalar_prefetch=2,
        grid=(nj, nb),
        in_specs=[pl.BlockSpec((tm, d), lambda j, b, be, nu: (b, 0)),
                  pl.BlockSpec((None, None, d, tn), lambda j, b, be, nu: (layer, be[b], 0, j)),
                  pl.BlockSpec((None, None, d, tn), lambda j, b, be, nu: (layer, be[b], 0, nj + j)),
                  pl.BlockSpec((None, None, 1, tn), lambda j, b, be, nu: (layer, be[b], 0, j)),
                  pl.BlockSpec((None, None, 1, tn), lambda j, b, be, nu: (layer, be[b], 0, nj + j))],
        out_specs=pl.BlockSpec((tm, tn), lambda j, b, be, nu: (b, j)),
        scratch_shapes=[pltpu.VMEM((d, tn), BF16), pltpu.VMEM((d, tn), BF16)],
    )
    b3 = b_gu.reshape(depth, n_e, 1, two_ff)
    return pl.pallas_call(
        _moe_gu_kernel,
        grid_spec=grid_spec,
        out_shape=jax.ShapeDtypeStruct((n_slots, d_ff), BF16),
        compiler_params=_params(("arbitrary", "arbitrary")),
        name="moe_gate_up",
    )(block_expert, nused, xs, w_gu, w_gu, b3, b3)


def _moe_down_kernel(be_ref, nused_ref, h_ref, w_ref, bias_ref, o_ref, w_bf):
    b = pl.program_id(1)

    @pl.when(_expert_changed(be_ref, b))
    def _():
        w_bf[...] = w_ref[...].astype(BF16)

    @pl.when(b < nused_ref[0])
    def _():
        o_ref[...] = jnp.dot(h_ref[...], w_bf[...], preferred_element_type=F32) + bias_ref[...]

    @pl.when(b >= nused_ref[0])
    def _():
        o_ref[...] = jnp.zeros(o_ref.shape, o_ref.dtype)


def moe_down(hdn, w_down, b_down, layer, block_expert, nused):
    n_slots, d_ff = hdn.shape
    depth, n_e, _, d = w_down.shape
    tm = MOE_TM
    tn = _tile(d, MOE_DOWN_TN)
    nb = n_slots // tm
    grid_spec = pltpu.PrefetchScalarGridSpec(
        num_scalar_prefetch=2,
        grid=(d // tn, nb),
        in_specs=[pl.BlockSpec((tm, d_ff), lambda j, b, be, nu: (b, 0)),
                  pl.BlockSpec((None, None, d_ff, tn), lambda j, b, be, nu: (layer, be[b], 0, j)),
                  pl.BlockSpec((None, None, 1, tn), lambda j, b, be, nu: (layer, be[b], 0, j))],
        out_specs=pl.BlockSpec((tm, tn), lambda j, b, be, nu: (b, j)),
        scratch_shapes=[pltpu.VMEM((d_ff, tn), BF16)],
    )
    return pl.pallas_call(
        _moe_down_kernel,
        grid_spec=grid_spec,
        out_shape=jax.ShapeDtypeStruct((n_slots, d), F32),
        compiler_params=_params(("arbitrary", "arbitrary")),
        name="moe_down",
    )(block_expert, nused, hdn, w_down, b_down.reshape(depth, n_e, 1, d))


def _moe_combine_kernel(pos_ref, pos_next_ref, gates_ref, x_ref, y_ref, gain_ref, g_ref, o_ref, buf, sem,
                        *, tm, nb):
    i = pl.program_id(0)
    slot = i % 2

    def start_rows(pos, dst_slot):
        def issue(r, carry):
            for k in range(TOP_K):
                pltpu.make_async_copy(y_ref.at[pl.ds(pos[r * TOP_K + k], 1), :],
                                      buf.at[dst_slot, k, pl.ds(r, 1), :], sem.at[dst_slot]).start()
            return carry
        lax.fori_loop(0, tm, issue, 0)

    @pl.when(i == 0)
    def _():
        start_rows(pos_ref, 0)

    @pl.when(i + 1 < nb)
    def _():
        start_rows(pos_next_ref, 1 - slot)

    for k in range(TOP_K):
        pltpu.make_async_copy(y_ref.at[pl.ds(0, tm), :], buf.at[slot, k], sem.at[slot]).wait()
    gates = gates_ref[...]
    f = buf[slot, 0] * gates[:, 0:1]
    for k in range(1, TOP_K):
        f = f + buf[slot, k] * gates[:, k:k + 1]
    y = f * lax.rsqrt(jnp.mean(f * f, axis=-1, keepdims=True) + RMS_EPS) * gain_ref[...]
    o_ref[...] = x_ref[...] + g_ref[...] * y


def moe_combine_resid(x, y, pos, gates, gain, gate_mod, n_prompt, rows_per_sample):
    n, d = x.shape
    tm = _tile(math.gcd(n_prompt, rows_per_sample), MOE_COMBINE_TM)
    seg = functools.partial(_seg_of_block, tm=tm, n_prompt=n_prompt, rows_per_sample=rows_per_sample)
    s = gate_mod.shape[0]
    nb = n // tm
    kern = functools.partial(_moe_combine_kernel, tm=tm, nb=nb)
    pos3 = pos.reshape(nb, 1, tm * TOP_K)
    return pl.pallas_call(
        kern,
        grid=(nb,),
        in_specs=[pl.BlockSpec((None, None, tm * TOP_K), lambda i: (i, 0, 0), memory_space=pltpu.SMEM),
                  pl.BlockSpec((None, None, tm * TOP_K), lambda i: (jnp.minimum(i + 1, nb - 1), 0, 0),
                               memory_space=pltpu.SMEM),
                  pl.BlockSpec((tm, TOP_K), lambda i: (i, 0)),
                  pl.BlockSpec((tm, d), lambda i: (i, 0)),
                  pl.BlockSpec(memory_space=pl.ANY),
                  pl.BlockSpec((1, d), lambda i: (0, 0)),
                  pl.BlockSpec((None, 1, d), lambda i: (seg(i), 0, 0))],
        out_specs=pl.BlockSpec((tm, d), lambda i: (i, 0)),
        out_shape=jax.ShapeDtypeStruct((n, d), F32),
        scratch_shapes=[pltpu.VMEM((2, TOP_K, tm, d), F32), pltpu.SemaphoreType.DMA((2,))],
        compiler_params=_params(("arbitrary",)),
        name="moe_combine",
    )(pos3, pos3, gates, x, y, gain.reshape(1, d), gate_mod.reshape(s, 1, d))


def moe_route(h, w_r, b_r):
    n, d = h.shape
    w_pad = jnp.pad(w_r, ((0, 0), (0, LANE - N_EXPERTS))).astype(BF16)
    logits = mm(h, w_pad, tm=512, tn=LANE)[:, :N_EXPERTS] + b_r
    top_logit, top_idx = lax.top_k(logits, TOP_K)
    gates = jax.nn.softmax(top_logit, axis=-1)
    a = n * TOP_K
    flat_e = top_idx.reshape(a)
    onehot = (flat_e[:, None] == jnp.arange(N_EXPERTS, dtype=flat_e.dtype)[None, :]).astype(jnp.int32)
    csum = jnp.cumsum(onehot, axis=0)
    rank = jnp.take_along_axis(csum, flat_e[:, None], axis=1)[:, 0] - 1
    counts = csum[-1]
    padded = (counts + MOE_TM - 1) // MOE_TM * MOE_TM
    pad_end = jnp.cumsum(padded)
    pad_start = pad_end - padded
    dest = (pad_start[flat_e] + rank).astype(jnp.int32)
    n_blocks = a // MOE_TM + N_EXPERTS
    n_slots = n_blocks * MOE_TM
    slot_tok = jnp.zeros((n_slots,), jnp.int32).at[dest].set(jnp.arange(a, dtype=jnp.int32) // TOP_K)
    block_expert = jnp.minimum(
        jnp.searchsorted(pad_end, jnp.arange(n_blocks, dtype=jnp.int32) * MOE_TM, side='right'),
        N_EXPERTS - 1).astype(jnp.int32)
    nused = (pad_end[-1] // MOE_TM).astype(jnp.int32).reshape(1)
    return slot_tok, block_expert, nused, dest.reshape(n, TOP_K), gates, n_slots


def moe_block(x, h, w_r, b_r, w_gu, b_gu, w_down, b_down, layer, gain, gate_mod, n_prompt, rows_per_sample):
    slot_tok, block_expert, nused, pos, gates, n_slots = moe_route(h, w_r, b_r)
    xs = moe_gather(h, slot_tok, nused, n_slots)
    hdn = moe_gate_up(xs, w_gu, b_gu, layer, block_expert, nused)
    y = moe_down(hdn, w_down, b_down, layer, block_expert, nused)
    return moe_combine_resid(x, y, pos, gates, gain, gate_mod, n_prompt, rows_per_sample)


def _rms(x, gain):
    return x * lax.rsqrt(jnp.mean(x * x, axis=-1, keepdims=True) + RMS_EPS) * gain


def _rope_tables(n_tokens, rot_dim):
    rows = n_tokens // GRID_W
    row = jnp.repeat(jnp.arange(rows, dtype=F32), GRID_W)
    col = jnp.tile(jnp.arange(GRID_W, dtype=F32), rows)
    axis_dim = rot_dim // 2
    inv_freq = ROPE_THETA ** (-jnp.arange(0, axis_dim, 2, dtype=F32) / axis_dim)
    ang = jnp.concatenate([row[:, None] * inv_freq, col[:, None] * inv_freq], axis=-1)
    cos, sin = jnp.cos(ang), jnp.sin(ang)
    return jnp.concatenate([cos, cos], axis=-1), jnp.concatenate([-sin, sin], axis=-1)


def _rope(x, c_tab, s_tab):
    shape = (c_tab.shape[0],) + (1,) * (x.ndim - 3) + (c_tab.shape[1],)
    return x * c_tab.reshape(shape) + jnp.roll(x, x.shape[-1] // 2, axis=-1) * s_tab.reshape(shape)


def _rope_sample_rows(x, tabs, n_p, bs, ts):
    xs = x[n_p:].reshape((bs, ts) + x.shape[1:])
    return jnp.concatenate([x[:n_p], _rope(xs, *tabs).reshape((bs * ts,) + x.shape[1:])], axis=0)


def _keys(bs, cache, own_sample, own_prompt):
    w = own_prompt.shape[-1]
    s = jnp.concatenate([cache.reshape(bs, -1, w), own_sample.reshape(bs, -1, w)], axis=1)
    return jnp.concatenate([s.reshape(-1, w), own_prompt], axis=0)


def _attend_groups(make_parts, *, dims, heads, dv, scale, out_dtype):
    bp, tp, bs, ts, past = dims
    q_parts, k_parts, v_part = make_parts
    o_p = attention(q_parts, k_parts, v_part, batch=bp, heads=heads, tq_total=tp, tk_total=tp,
                    q_row0=0, k_row0=bs * (past + ts), dv=dv, scale=scale, out_dtype=out_dtype)
    o_s = attention(q_parts, k_parts, v_part, batch=bs, heads=heads, tq_total=ts, tk_total=past + ts,
                    q_row0=bp * tp, k_row0=0, dv=dv, scale=scale, out_dtype=out_dtype)
    return jnp.concatenate([o_p, o_s], axis=0)


def even_mixer(h, dims, ropes, row_ropes, caches, w_in, q_norm, w_uq, kv_norm, w_ukv, lam_params, subln, w_out,
               lambda_init):
    bp, tp, bs, ts, past = dims
    n_p = bp * tp
    n = h.shape[0]
    c1 = MLA_Q_RANK + MLA_KV_RANK
    c2 = c1 + MLA_ROPE_DIM
    c3 = c2 + 2 * DIFF_QK_WIDTH
    p128, p64 = _deinterleave_matrix(LANE), _deinterleave_matrix(MLA_ROPE_DIM)
    w_in_bf = w_in.astype(BF16)
    a = mm(h, w_in_bf[:, :c1])
    w_kr = jnp.pad(w_in[:, c1:c2], ((0, 0), (0, LANE - MLA_ROPE_DIM)))
    kr = mm(h, permute_lanes(w_kr, p64), tn=LANE)[:, :MLA_ROPE_DIM]
    kr_own = mm(h, w_kr.astype(BF16), tn=LANE, rows=n_p)[:, :MLA_ROPE_DIM]
    dqk = mm(h, permute_lanes(w_in[:, c2:c3], p128))
    dqk_r = rope_cast(dqk, 0, 2 * DIFF_QK_WIDTH, *row_ropes[1])
    dk_own = mm(h, w_in_bf[:, c2 + DIFF_QK_WIDTH:c3], rows=n_p)
    dv_bf = mm(h, w_in_bf[:, c3:], out_dtype=BF16)
    dv_own = mm(h, w_in_bf[:, c3:], rows=n_p)
    cq, ckv = a[:, :MLA_Q_RANK], a[:, MLA_Q_RANK:]
    w_uq3 = w_uq.reshape(MLA_Q_RANK, MLA_HEADS, MLA_NOPE_DIM + MLA_ROPE_DIM)
    cq_n = _rms(cq, q_norm).astype(BF16)
    q_nope = mm(cq_n, w_uq3[:, :, :MLA_NOPE_DIM].reshape(MLA_Q_RANK, -1).astype(BF16), out_dtype=BF16)
    q_rope = mm(cq_n, permute_lanes(w_uq3[:, :, MLA_NOPE_DIM:].reshape(MLA_Q_RANK, -1), p64))
    q_rope = q_rope.reshape(n, MLA_HEADS, MLA_ROPE_DIM)
    ckv = _rms(ckv, kv_norm)

    q_rope = _rope_sample_rows(q_rope, ropes[0], n_p, bs, ts)
    kr_r = _rope_sample_rows(kr, ropes[0], n_p, bs, ts).astype(BF16)
    dk_r = dqk_r[:, DIFF_QK_WIDTH:]

    own = (ckv[:n_p].reshape(bp, tp, MLA_KV_RANK), kr_own.reshape(bp, tp, MLA_ROPE_DIM),
           dk_own.reshape(bp, tp, DIFF_HEADS, 2 * DIFF_HEAD_DIM),
           dv_own.reshape(bp, tp, DIFF_HEADS, DIFF_V_DIM))

    c_ckv, c_kr, c_dk, c_dv = caches
    ckv_bf = ckv.astype(BF16)
    ckv_keys = _keys(bs, c_ckv.astype(BF16), ckv_bf[n_p:], ckv_bf[:n_p])
    kv = mm(ckv_keys, w_ukv.astype(BF16), out_dtype=BF16)
    c_kr_perm = c_kr[..., _deinterleave_src(MLA_ROPE_DIM)[:MLA_ROPE_DIM]].astype(BF16)
    kr_keys = jnp.pad(_keys(bs, c_kr_perm, kr_r[n_p:], kr_r[:n_p]), ((0, 0), (0, LANE - MLA_ROPE_DIM)))
    q_rope_pad = jnp.pad(q_rope, ((0, 0), (0, 0), (0, LANE - MLA_ROPE_DIM))).reshape(n, -1).astype(BF16)
    a_out = _attend_groups(
        ([(q_nope, lambda hh: hh), (q_rope_pad, lambda hh: hh)],
         [(kv, lambda hh: 2 * hh), (kr_keys, lambda hh: 0)],
         (kv, lambda hh: 2 * hh + 1)),
        dims=dims, heads=MLA_HEADS, dv=MLA_V_DIM,
        scale=(MLA_NOPE_DIM + MLA_ROPE_DIM) ** -0.5, out_dtype=BF16)

    c_dk_perm = permute_lanes(c_dk.reshape(bs * past, -1), p128)
    dk_keys = _keys(bs, c_dk_perm, dk_r[n_p:], dk_r[:n_p])
    dv_keys = _keys(bs, c_dv.astype(BF16), dv_bf[n_p:], dv_bf[:n_p])
    o2 = _attend_groups(
        ([(dqk_r, lambda hh: hh)],
         [(dk_keys, lambda hh: hh)],
         (dv_keys, lambda hh: hh // 2)),
        dims=dims, heads=2 * DIFF_HEADS, dv=DIFF_V_DIM,
        scale=DIFF_HEAD_DIM ** -0.5, out_dtype=F32).reshape(n, DIFF_HEADS, 2 * DIFF_V_DIM)
    lq1, lk1, lq2, lk2 = lam_params.astype(F32)
    lam = jnp.exp(jnp.sum(lq1 * lk1)) - jnp.exp(jnp.sum(lq2 * lk2)) + lambda_init
    b_out = o2[:, :, :DIFF_V_DIM] - lam * o2[:, :, DIFF_V_DIM:]
    b_out = _rms(b_out, subln) * (1.0 - lambda_init)
    merged = jnp.concatenate([a_out, b_out.reshape(n, -1).astype(BF16)], axis=-1)
    return mm(merged, w_out.astype(BF16)), own


def odd_mixer(h, dims, ropes, row_ropes, caches, w_in, q_norm, k_norm, w_out):
    bp, tp, bs, ts, past = dims
    n_p = bp * tp
    n = h.shape[0]
    nq = GQA_Q_HEADS * GQA_HEAD_DIM
    nk = GQA_KV_HEADS * GQA_HEAD_DIM
    p128 = _deinterleave_matrix(LANE)
    src = _deinterleave_src(LANE)
    w_in_bf = w_in.astype(BF16)
    qk = mm(h, permute_lanes(w_in[:, :nq + nk], p128))
    k_own = _rms(mm(h, w_in_bf[:, nq:nq + nk], rows=n_p).reshape(n_p, GQA_KV_HEADS, GQA_HEAD_DIM), k_norm)
    q_r = rope_cast(qk, 0, nq, *row_ropes[2], gain=q_norm[src])
    k_r = rope_cast(qk, nq, nk, *row_ropes[2], gain=k_norm[src])
    v_bf = mm(h, w_in_bf[:, nq + nk:], out_dtype=BF16)
    v_own = mm(h, w_in_bf[:, nq + nk:], rows=n_p)
    own = (k_own.reshape(bp, tp, GQA_KV_HEADS, GQA_HEAD_DIM),
           v_own.reshape(bp, tp, GQA_KV_HEADS, GQA_HEAD_DIM))
    c_k, c_v = caches
    k_keys = _keys(bs, permute_lanes(c_k.reshape(bs * past, -1), p128), k_r[n_p:], k_r[:n_p])
    v_keys = _keys(bs, c_v.astype(BF16), v_bf[n_p:], v_bf[:n_p])
    rep = GQA_Q_HEADS // GQA_KV_HEADS
    o = _attend_groups(
        ([(q_r, lambda hh: hh)],
         [(k_keys, lambda hh: hh // rep)],
         (v_keys, lambda hh: hh // rep)),
        dims=dims, heads=GQA_Q_HEADS, dv=GQA_HEAD_DIM, scale=GQA_HEAD_DIM ** -0.5, out_dtype=BF16)
    return mm(o, w_out.astype(BF16)), own


def kernel(x_prompt, x_sample, cache_mla_ckv, cache_mla_krope, cache_diff_k, cache_diff_v, cache_gqa_k, cache_gqa_v, c, c_ctx, mod_w, mod_b, norm_gains, even_w_in, mla_q_norm, mla_w_uq, mla_kv_norm, mla_w_ukv, diff_lambda, diff_subln, even_w_out, odd_w_in, gqa_q_norm, gqa_k_norm, odd_w_out, router_w, router_b, moe_w_gu, moe_b_gu, moe_w_down, moe_b_down):
    bp, tp, d = x_prompt.shape
    bs, ts, _ = x_sample.shape
    past = cache_mla_ckv.shape[2]
    depth = mod_w.shape[0]
    n_p = bp * tp
    dims = (bp, tp, bs, ts, past)
    x = jnp.concatenate([x_prompt.reshape(n_p, d), x_sample.reshape(bs * ts, d)], axis=0)
    cond = jnp.concatenate([c_ctx[None, :], c], axis=0)
    n_seg = cond.shape[0]
    cond_act = jnp.pad(jax.nn.silu(cond), ((0, (-n_seg) % 16), (0, 0))).astype(BF16)
    ropes = (_rope_tables(ts, MLA_ROPE_DIM), _rope_tables(ts, DIFF_HEAD_DIM), _rope_tables(ts, GQA_HEAD_DIM))

    def per_row(tabs):
        c_tab, s_tab = tabs
        return (jnp.concatenate([jnp.ones((n_p, LANE), F32), jnp.tile(c_tab, (bs, 1))], axis=0),
                jnp.concatenate([jnp.zeros((n_p, LANE), F32), jnp.tile(s_tab, (bs, 1))], axis=0))

    row_ropes = (None, per_row(ropes[1]), per_row(ropes[2]))

    even_states, odd_states = [], []
    for layer in range(depth):
        mod = mm(cond_act, mod_w, w_index=layer)[:n_seg] + mod_b[layer]
        sh_m, sc_m, g_m, sh_f, sc_f, g_f = jnp.split(mod, 6, axis=-1)
        gains = norm_gains[layer]
        h = norm_mod(x, gains[0], sc_m, sh_m, n_p, ts, BF16)
        i = layer // 2
        if layer % 2 == 0:
            caches = (cache_mla_ckv[:, i], cache_mla_krope[:, i], cache_diff_k[:, i], cache_diff_v[:, i])
            lambda_init = 0.8 - 0.6 * math.exp(-0.3 * layer)
            m, own = even_mixer(h, dims, ropes, row_ropes, caches, even_w_in[i], mla_q_norm[i], mla_w_uq[i],
                                mla_kv_norm[i], mla_w_ukv[i], diff_lambda[i], diff_subln[i], even_w_out[i],
                                lambda_init)
            even_states.append(own)
        else:
            caches = (cache_gqa_k[:, i], cache_gqa_v[:, i])
            m, own = odd_mixer(h, dims, ropes, row_ropes, caches, odd_w_in[i], gqa_q_norm[i], gqa_k_norm[i], odd_w_out[i])
            odd_states.append(own)
        x, h = resid_norm_mod(x, m, gains[1], g_m, gains[2], sc_f, sh_f, n_p, ts)
        x = moe_block(x, h, router_w[layer], router_b[layer], moe_w_gu, moe_b_gu, moe_w_down, moe_b_down,
                      layer, gains[3], g_f, n_p, ts)

    y_prompt = x[:n_p].reshape(bp, tp, d)
    y_sample = x[n_p:].reshape(bs, ts, d)
    return (y_prompt, y_sample,
            jnp.stack([s[0] for s in even_states], axis=1),
            jnp.stack([s[1] for s in even_states], axis=1),
            jnp.stack([s[2] for s in even_states], axis=1),
            jnp.stack([s[3] for s in even_states], axis=1),
            jnp.stack([s[0] for s in odd_states], axis=1),
            jnp.stack([s[1] for s in odd_states], axis=1))
```

```python
import functools
import math

import jax
import jax.numpy as jnp
import numpy as np
from jax import lax
from jax.experimental import pallas as pl
from jax.experimental.pallas import tpu as pltpu

F32 = jnp.float32
BF16 = jnp.bfloat16

GRID_W = 64
ROPE_THETA = 10000.0
RMS_EPS = 1e-6
MLA_HEADS = 16
MLA_Q_RANK = 1024
MLA_KV_RANK = 512
MLA_NOPE_DIM = 128
MLA_ROPE_DIM = 64
MLA_V_DIM = 128
DIFF_HEADS = 8
DIFF_HEAD_DIM = 128
DIFF_V_DIM = 2 * DIFF_HEAD_DIM
DIFF_QK_WIDTH = DIFF_HEADS * 2 * DIFF_HEAD_DIM
GQA_Q_HEADS = 32
GQA_KV_HEADS = 8
GQA_HEAD_DIM = 128
N_EXPERTS = 32
TOP_K = 4
SWIGLU_LIMIT = 7.0
SWIGLU_ALPHA = 1.702

LANE = 128
ATTN_TQ = 2048
ATTN_SUB = 256
VMEM_LIMIT = 56 * 1024 * 1024


def _tile(dim, pref):
    t = min(pref, dim)
    while dim % t:
        t //= 2
    return t


def _params(sem):
    return pltpu.CompilerParams(dimension_semantics=sem, vmem_limit_bytes=VMEM_LIMIT)


def _mm_kernel(x_ref, w_ref, o_ref):
    o_ref[...] = jnp.dot(x_ref[...].astype(BF16), w_ref[...].astype(BF16),
                         preferred_element_type=F32).astype(o_ref.dtype)


def mm(x, w, out_dtype=F32, tm=1024, tn=512, w_index=None, rows=None):
    m, k = x.shape
    m = m if rows is None else rows
    n = w.shape[-1]
    tm = _tile(m, tm)
    tn = _tile(n, tn)
    if w_index is None:
        w_spec = pl.BlockSpec((k, tn), lambda i, j: (0, j))
    else:
        w_spec = pl.BlockSpec((None, k, tn), lambda i, j: (w_index, 0, j))
    return pl.pallas_call(
        _mm_kernel,
        grid=(m // tm, n // tn),
        in_specs=[pl.BlockSpec((tm, k), lambda i, j: (i, 0)), w_spec],
        out_specs=pl.BlockSpec((tm, tn), lambda i, j: (i, j)),
        out_shape=jax.ShapeDtypeStruct((m, n), out_dtype),
        compiler_params=_params(("parallel", "parallel")),
        name="mm",
    )(x, w)


def _deinterleave_src(blk):
    j = np.arange(LANE)
    o = j % blk
    return (j // blk) * blk + np.where(o < blk // 2, 2 * o, 2 * (o - blk // 2) + 1)


def _deinterleave_matrix(blk):
    p = np.zeros((LANE, LANE), np.float32)
    p[_deinterleave_src(blk), np.arange(LANE)] = 1.0
    return jnp.asarray(p, BF16)


def _permute_kernel(w_ref, p_ref, o_ref):
    o_ref[...] = jnp.dot(w_ref[...].astype(BF16), p_ref[...], preferred_element_type=F32).astype(o_ref.dtype)


def permute_lanes(w, perm):
    r, c = w.shape
    tm = _tile(r, 1024)
    return pl.pallas_call(
        _permute_kernel,
        grid=(r // tm, c // LANE),
        in_specs=[pl.BlockSpec((tm, LANE), lambda i, j: (i, j)),
                  pl.BlockSpec((LANE, LANE), lambda i, j: (0, 0))],
        out_specs=pl.BlockSpec((tm, LANE), lambda i, j: (i, j)),
        out_shape=jax.ShapeDtypeStruct((r, c), BF16),
        compiler_params=_params(("parallel", "parallel")),
        name="permute_lanes",
    )(w, perm)


def _seg_of_block(i, tm, n_prompt, rows_per_sample):
    pb = n_prompt // tm
    return jnp.where(i < pb, 0, 1 + (i - pb) // (rows_per_sample // tm))


def _norm_mod_kernel(x_ref, gain_ref, sc_ref, sh_ref, o_ref):
    x = x_ref[...]
    y = x * lax.rsqrt(jnp.mean(x * x, axis=-1, keepdims=True) + RMS_EPS) * gain_ref[...]
    o_ref[...] = (y * (1.0 + sc_ref[...]) + sh_ref[...]).astype(o_ref.dtype)


def norm_mod(x, gain, scale, shift, n_prompt, rows_per_sample, out_dtype):
    n, d = x.shape
    tm = _tile(math.gcd(n_prompt, rows_per_sample), 256)
    seg = functools.partial(_seg_of_block, tm=tm, n_prompt=n_prompt, rows_per_sample=rows_per_sample)
    s = scale.shape[0]
    return pl.pallas_call(
        _norm_mod_kernel,
        grid=(n // tm,),
        in_specs=[pl.BlockSpec((tm, d), lambda i: (i, 0)),
                  pl.BlockSpec((1, d), lambda i: (0, 0)),
                  pl.BlockSpec((None, 1, d), lambda i: (seg(i), 0, 0)),
                  pl.BlockSpec((None, 1, d), lambda i: (seg(i), 0, 0))],
        out_specs=pl.BlockSpec((tm, d), lambda i: (i, 0)),
        out_shape=jax.ShapeDtypeStruct((n, d), out_dtype),
        compiler_params=_params(("parallel",)),
        name="norm_mod",
    )(x, gain.reshape(1, d), scale.reshape(s, 1, d), shift.reshape(s, 1, d))


def _resid_norm_mod_kernel(x_ref, m_ref, gain_ref, g_ref, gain2_ref, sc_ref, sh_ref, xo_ref, ho_ref):
    m = m_ref[...]
    y = m * lax.rsqrt(jnp.mean(m * m, axis=-1, keepdims=True) + RMS_EPS) * gain_ref[...]
    x = x_ref[...] + g_ref[...] * y
    xo_ref[...] = x
    z = x * lax.rsqrt(jnp.mean(x * x, axis=-1, keepdims=True) + RMS_EPS) * gain2_ref[...]
    ho_ref[...] = (z * (1.0 + sc_ref[...]) + sh_ref[...]).astype(ho_ref.dtype)


def resid_norm_mod(x, m, gain, gate, gain2, scale, shift, n_prompt, rows_per_sample):
    n, d = x.shape
    tm = _tile(math.gcd(n_prompt, rows_per_sample), 256)
    seg = functools.partial(_seg_of_block, tm=tm, n_prompt=n_prompt, rows_per_sample=rows_per_sample)
    s = gate.shape[0]
    row = pl.BlockSpec((tm, d), lambda i: (i, 0))
    vec = pl.BlockSpec((1, d), lambda i: (0, 0))
    seg_vec = pl.BlockSpec((None, 1, d), lambda i: (seg(i), 0, 0))
    return pl.pallas_call(
        _resid_norm_mod_kernel,
        grid=(n // tm,),
        in_specs=[row, row, vec, seg_vec, vec, seg_vec, seg_vec],
        out_specs=[row, row],
        out_shape=[jax.ShapeDtypeStruct((n, d), F32), jax.ShapeDtypeStruct((n, d), F32)],
        compiler_params=_params(("parallel",)),
        name="resid_norm_mod",
    )(x, m, gain.reshape(1, d), gate.reshape(s, 1, d), gain2.reshape(1, d),
      scale.reshape(s, 1, d), shift.reshape(s, 1, d))


def _rope_cast_kernel(*refs, has_gain):
    x_ref, c_ref, s_ref = refs[:3]
    o_ref = refs[-1]
    c = c_ref[...]
    s = s_ref[...]
    for k in range(x_ref.shape[1] // LANE):
        a = x_ref[:, k * LANE:(k + 1) * LANE]
        if has_gain:
            a = a * lax.rsqrt(jnp.mean(a * a, axis=-1, keepdims=True) + RMS_EPS) * refs[3][...]
        o_ref[:, k * LANE:(k + 1) * LANE] = (a * c + pltpu.roll(a, LANE // 2, axis=1) * s).astype(o_ref.dtype)


def rope_cast(x, col0, width, c_rows, s_rows, gain=None):
    n = x.shape[0]
    tm = _tile(n, 512)
    tw = _tile(width, 512)
    assert col0 % tw == 0
    cb = col0 // tw
    in_specs = [pl.BlockSpec((tm, tw), lambda i, j: (i, cb + j)),
                pl.BlockSpec((tm, LANE), lambda i, j: (i, 0)),
                pl.BlockSpec((tm, LANE), lambda i, j: (i, 0))]
    args = [x, c_rows, s_rows]
    if gain is not None:
        in_specs.append(pl.BlockSpec((1, LANE), lambda i, j: (0, 0)))
        args.append(gain.reshape(1, LANE))
    return pl.pallas_call(
        functools.partial(_rope_cast_kernel, has_gain=gain is not None),
        grid=(n // tm, width // tw),
        in_specs=in_specs,
        out_specs=pl.BlockSpec((tm, tw), lambda i, j: (i, j)),
        out_shape=jax.ShapeDtypeStruct((n, width), BF16),
        compiler_params=_params(("parallel", "parallel")),
        name="rope_cast",
    )(*args)


def _attn_kernel(*refs, n_q, n_k, exp2_scale):
    q_refs = refs[:n_q]
    k_refs = refs[n_q:n_q + n_k]
    v_ref = refs[n_q + n_k]
    o_ref = refs[n_q + n_k + 1]

    if n_k > 1:
        kcat = refs[n_q + n_k + 2]

        @pl.when(pl.program_id(2) == 0)
        def _():
            for idx, r in enumerate(k_refs):
                kcat[:, idx * LANE:(idx + 1) * LANE] = r[...]
        k = kcat[...]
    else:
        k = k_refs[0][...]
    v = v_ref[...]
    tq = o_ref.shape[0]
    sub = min(tq, ATTN_SUB)
    for r0 in range(0, tq, sub):
        qs = [r[r0:r0 + sub, :] for r in q_refs]
        q = jnp.concatenate(qs, axis=1) if n_q > 1 else qs[0]
        s = lax.dot_general(q, k, (((1,), (1,)), ((), ())), preferred_element_type=F32)
        m = jnp.max(s, axis=1, keepdims=True)
        p = jnp.exp2((s - m) * exp2_scale)
        l = jnp.sum(p, axis=1, keepdims=True)
        o = jnp.dot(p.astype(BF16), v, preferred_element_type=F32)
        o_ref[r0:r0 + sub, :] = (o / l).astype(o_ref.dtype)


def attention(q_parts, k_parts, v_part, *, batch, heads, tq_total, tk_total, q_row0, k_row0,
              dv, scale, out_dtype):
    tq = _tile(tq_total, ATTN_TQ)
    nq = tq_total // tq
    assert q_row0 % tq == 0 and k_row0 % tk_total == 0
    qb0, kb0 = q_row0 // tq, k_row0 // tk_total

    def q_spec(fn):
        return pl.BlockSpec((tq, LANE), lambda b, h, i: (qb0 + b * nq + i, fn(h)))

    def k_spec(fn, width):
        return pl.BlockSpec((tk_total, width), lambda b, h, i: (kb0 + b, fn(h)))

    in_specs = ([q_spec(fn) for _, fn in q_parts] + [k_spec(fn, LANE) for _, fn in k_parts]
                + [k_spec(v_part[1], dv)])
    n_k = len(k_parts)
    kern = functools.partial(_attn_kernel, n_q=len(q_parts), n_k=n_k, exp2_scale=scale * math.log2(math.e))
    scratch = [pltpu.VMEM((tk_total, n_k * LANE), BF16)] if n_k > 1 else []
    return pl.pallas_call(
        kern,
        grid=(batch, heads, nq),
        in_specs=in_specs,
        out_specs=pl.BlockSpec((tq, dv), lambda b, h, i: (b * nq + i, h)),
        out_shape=jax.ShapeDtypeStruct((batch * tq_total, heads * dv), out_dtype),
        scratch_shapes=scratch,
        compiler_params=_params(("parallel", "parallel", "arbitrary")),
        name="attention",
    )(*[a for a, _ in q_parts], *[a for a, _ in k_parts], v_part[0])


MOE_TM = 512
MOE_GATHER_TM = 256
MOE_COMBINE_TM = 128
MOE_DOWN_TN = 2048
MOE_GU_TN = 512


def _moe_gather_kernel(tok_ref, tok_next_ref, nused_ref, h_ref, o_ref, buf, sem, *, tm, blocks_per_moe_block):
    b = pl.program_id(0)
    n_used = nused_ref[0] * blocks_per_moe_block
    slot = b % 2

    def start_rows(toks, dst_slot):
        def issue(r2, carry):
            for pri in range(2):
                r = 2 * r2 + pri
                pltpu.make_async_copy(h_ref.at[pl.ds(toks[r], 1), :], buf.at[dst_slot, pl.ds(r, 1), :],
                                      sem.at[dst_slot]).start(priority=pri)
            return carry
        lax.fori_loop(0, tm // 2, issue, 0)

    @pl.when(jnp.logical_and(b == 0, n_used > 0))
    def _():
        start_rows(tok_ref, 0)

    @pl.when(b + 1 < n_used)
    def _():
        start_rows(tok_next_ref, 1 - slot)

    @pl.when(b < n_used)
    def _():
        pltpu.make_async_copy(h_ref.at[pl.ds(0, tm), :], buf.at[slot], sem.at[slot]).wait()
        o_ref[...] = buf[slot].astype(o_ref.dtype)

    @pl.when(b >= n_used)
    def _():
        o_ref[...] = jnp.zeros(o_ref.shape, o_ref.dtype)


def moe_gather(h, slot_tok, nused, n_slots):
    n, d = h.shape
    tm = MOE_GATHER_TM
    nb = n_slots // tm
    kern = functools.partial(_moe_gather_kernel, tm=tm, blocks_per_moe_block=MOE_TM // tm)
    tok3 = slot_tok.reshape(nb, 1, tm)
    return pl.pallas_call(
        kern,
        grid=(nb,),
        in_specs=[pl.BlockSpec((None, None, tm), lambda b: (b, 0, 0), memory_space=pltpu.SMEM),
                  pl.BlockSpec((None, None, tm), lambda b: (jnp.minimum(b + 1, nb - 1), 0, 0),
                               memory_space=pltpu.SMEM),
                  pl.BlockSpec(memory_space=pltpu.SMEM),
                  pl.BlockSpec(memory_space=pl.ANY)],
        out_specs=pl.BlockSpec((tm, d), lambda b: (b, 0)),
        out_shape=jax.ShapeDtypeStruct((n_slots, d), BF16),
        scratch_shapes=[pltpu.VMEM((2, tm, d), F32), pltpu.SemaphoreType.DMA((2,))],
        compiler_params=_params(("arbitrary",)),
        name="moe_gather",
    )(tok3, tok3, nused, h)


def _expert_changed(be_ref, b):
    return jnp.logical_or(b == 0, be_ref[b] != be_ref[jnp.maximum(b - 1, 0)])


def _moe_gu_kernel(be_ref, nused_ref, x_ref, wg_ref, wu_ref, bg_ref, bu_ref, o_ref, wg_bf, wu_bf):
    b = pl.program_id(1)

    @pl.when(_expert_changed(be_ref, b))
    def _():
        wg_bf[...] = wg_ref[...].astype(BF16)
        wu_bf[...] = wu_ref[...].astype(BF16)

    @pl.when(b < nused_ref[0])
    def _():
        x = x_ref[...]
        g = jnp.dot(x, wg_bf[...], preferred_element_type=F32) + bg_ref[...]
        u = jnp.dot(x, wu_bf[...], preferred_element_type=F32) + bu_ref[...]
        gate_h = jnp.minimum(g, SWIGLU_LIMIT)
        up_h = jnp.clip(u, -SWIGLU_LIMIT, SWIGLU_LIMIT)
        o_ref[...] = ((up_h + 1.0) * gate_h * jax.nn.sigmoid(SWIGLU_ALPHA * gate_h)).astype(o_ref.dtype)

    @pl.when(b >= nused_ref[0])
    def _():
        o_ref[...] = jnp.zeros(o_ref.shape, o_ref.dtype)


def moe_gate_up(xs, w_gu, b_gu, layer, block_expert, nused):
    n_slots, d = xs.shape
    depth, n_e, _, two_ff = w_gu.shape
    d_ff = two_ff // 2
    tm = MOE_TM
    tn = _tile(d_ff, MOE_GU_TN)
    nj = d_ff // tn
    nb = n_slots // tm
    grid_spec = pltpu.PrefetchScalarGridSpec(
        num_scalar_prefetch=2,
        grid=(nj, nb),
        in_specs=[pl.BlockSpec((tm, d), lambda j, b, be, nu: (b, 0)),
                  pl.BlockSpec((None, None, d, tn), lambda j, b, be, nu: (layer, be[b], 0, j)),
                  pl.BlockSpec((None, None, d, tn), lambda j, b, be, nu: (layer, be[b], 0, nj + j)),
                  pl.BlockSpec((None, None, 1, tn), lambda j, b, be, nu: (layer, be[b], 0, j)),
                  pl.BlockSpec((None, None, 1, tn), lambda j, b, be, nu: (layer, be[b], 0, nj + j))],
        out_specs=pl.BlockSpec((tm, tn), lambda j, b, be, nu: (b, j)),
        scratch_shapes=[pltpu.VMEM((d, tn), BF16), pltpu.VMEM((d, tn), BF16)],
    )
    b3 = b_gu.reshape(depth, n_e, 1, two_ff)
    return pl.pallas_call(
        _moe_gu_kernel,
        grid_spec=grid_spec,
        out_shape=jax.ShapeDtypeStruct((n_slots, d_ff), BF16),
        compiler_params=_params(("arbitrary", "arbitrary")),
        name="moe_gate_up",
    )(block_expert, nused, xs, w_gu, w_gu, b3, b3)


def _moe_down_kernel(be_ref, nused_ref, h_ref, w_ref, bias_ref, o_ref, w_bf):
    b = pl.program_id(1)

    @pl.when(_expert_changed(be_ref, b))
    def _():
        w_bf[...] = w_ref[...].astype(BF16)

    @pl.when(b < nused_ref[0])
    def _():
        o_ref[...] = jnp.dot(h_ref[...], w_bf[...], preferred_element_type=F32) + bias_ref[...]

    @pl.when(b >= nused_ref[0])
    def _():
        o_ref[...] = jnp.zeros(o_ref.shape, o_ref.dtype)


def moe_down(hdn, w_down, b_down, layer, block_expert, nused):
    n_slots, d_ff = hdn.shape
    depth, n_e, _, d = w_down.shape
    tm = MOE_TM
    tn = _tile(d, MOE_DOWN_TN)
    nb = n_slots // tm
    grid_spec = pltpu.PrefetchScalarGridSpec(
        num_scalar_prefetch=2,
        grid=(d // tn, nb),
        in_specs=[pl.BlockSpec((tm, d_ff), lambda j, b, be, nu: (b, 0)),
                  pl.BlockSpec((None, None, d_ff, tn), lambda j, b, be, nu: (layer, be[b], 0, j)),
                  pl.BlockSpec((None, None, 1, tn), lambda j, b, be, nu: (layer, be[b], 0, j))],
        out_specs=pl.BlockSpec((tm, tn), lambda j, b, be, nu: (b, j)),
        scratch_shapes=[pltpu.VMEM((d_ff, tn), BF16)],
    )
    return pl.pallas_call(
        _moe_down_kernel,
        grid_spec=grid_spec,
        out_shape=jax.ShapeDtypeStruct((n_slots, d), F32),
        compiler_params=_params(("arbitrary", "arbitrary")),
        name="moe_down",
    )(block_expert, nused, hdn, w_down, b_down.reshape(depth, n_e, 1, d))


def _moe_combine_kernel(pos_ref, pos_next_ref, gates_ref, x_ref, y_ref, gain_ref, g_ref, o_ref, buf, sem,
                        *, tm, nb):
    i = pl.program_id(0)
    slot = i % 2

    def start_rows(pos, dst_slot):
        def issue(r, carry):
            for k in range(TOP_K):
                pltpu.make_async_copy(y_ref.at[pl.ds(pos[r * TOP_K + k], 1), :],
                                      buf.at[dst_slot, k, pl.ds(r, 1), :], sem.at[dst_slot]).start(priority=k % 2)
            return carry
        lax.fori_loop(0, tm, issue, 0)

    @pl.when(i == 0)
    def _():
        start_rows(pos_ref, 0)

    @pl.when(i + 1 < nb)
    def _():
        start_rows(pos_next_ref, 1 - slot)

    for k in range(TOP_K):
        pltpu.make_async_copy(y_ref.at[pl.ds(0, tm), :], buf.at[slot, k], sem.at[slot]).wait()
    gates = gates_ref[...]
    f = buf[slot, 0] * gates[:, 0:1]
    for k in range(1, TOP_K):
        f = f + buf[slot, k] * gates[:, k:k + 1]
    y = f * lax.rsqrt(jnp.mean(f * f, axis=-1, keepdims=True) + RMS_EPS) * gain_ref[...]
    o_ref[...] = x_ref[...] + g_ref[...] * y


def moe_combine_resid(x, y, pos, gates, gain, gate_mod, n_prompt, rows_per_sample):
    n, d = x.shape
    tm = _tile(math.gcd(n_prompt, rows_per_sample), MOE_COMBINE_TM)
    seg = functools.partial(_seg_of_block, tm=tm, n_prompt=n_prompt, rows_per_sample=rows_per_sample)
    s = gate_mod.shape[0]
    nb = n // tm
    kern = functools.partial(_moe_combine_kernel, tm=tm, nb=nb)
    pos3 = pos.reshape(nb, 1, tm * TOP_K)
    return pl.pallas_call(
        kern,
        grid=(nb,),
        in_specs=[pl.BlockSpec((None, None, tm * TOP_K), lambda i: (i, 0, 0), memory_space=pltpu.SMEM),
                  pl.BlockSpec((None, None, tm * TOP_K), lambda i: (jnp.minimum(i + 1, nb - 1), 0, 0),
                               memory_space=pltpu.SMEM),
                  pl.BlockSpec((tm, TOP_K), lambda i: (i, 0)),
                  pl.BlockSpec((tm, d), lambda i: (i, 0)),
                  pl.BlockSpec(memory_space=pl.ANY),
                  pl.BlockSpec((1, d), lambda i: (0, 0)),
                  pl.BlockSpec((None, 1, d), lambda i: (seg(i), 0, 0))],
        out_specs=pl.BlockSpec((tm, d), lambda i: (i, 0)),
        out_shape=jax.ShapeDtypeStruct((n, d), F32),
        scratch_shapes=[pltpu.VMEM((2, TOP_K, tm, d), F32), pltpu.SemaphoreType.DMA((2,))],
        compiler_params=_params(("arbitrary",)),
        name="moe_combine",
    )(pos3, pos3, gates, x, y, gain.reshape(1, d), gate_mod.reshape(s, 1, d))


def moe_route(h, w_r, b_r):
    n, d = h.shape
    w_pad = jnp.pad(w_r, ((0, 0), (0, LANE - N_EXPERTS))).astype(BF16)
    logits = mm(h, w_pad, tm=512, tn=LANE)[:, :N_EXPERTS] + b_r
    top_logit, top_idx = lax.top_k(logits, TOP_K)
    gates = jax.nn.softmax(top_logit, axis=-1)
    a = n * TOP_K
    flat_e = top_idx.reshape(a)
    onehot = (flat_e[:, None] == jnp.arange(N_EXPERTS, dtype=flat_e.dtype)[None, :]).astype(jnp.int32)
    csum = jnp.cumsum(onehot, axis=0)
    rank = jnp.take_along_axis(csum, flat_e[:, None], axis=1)[:, 0] - 1
    counts = csum[-1]
    padded = (counts + MOE_TM - 1) // MOE_TM * MOE_TM
    pad_end = jnp.cumsum(padded)
    pad_start = pad_end - padded
    dest = (pad_start[flat_e] + rank).astype(jnp.int32)
    n_blocks = a // MOE_TM + N_EXPERTS
    n_slots = n_blocks * MOE_TM
    slot_tok = jnp.zeros((n_slots,), jnp.int32).at[dest].set(jnp.arange(a, dtype=jnp.int32) // TOP_K)
    block_expert = jnp.minimum(
        jnp.searchsorted(pad_end, jnp.arange(n_blocks, dtype=jnp.int32) * MOE_TM, side='right'),
        N_EXPERTS - 1).astype(jnp.int32)
    nused = (pad_end[-1] // MOE_TM).astype(jnp.int32).reshape(1)
    return slot_tok, block_expert, nused, dest.reshape(n, TOP_K), gates, n_slots


def moe_block(x, h, w_r, b_r, w_gu, b_gu, w_down, b_down, layer, gain, gate_mod, n_prompt, rows_per_sample):
    slot_tok, block_expert, nused, pos, gates, n_slots = moe_route(h, w_r, b_r)
    xs = moe_gather(h, slot_tok, nused, n_slots)
    hdn = moe_gate_up(xs, w_gu, b_gu, layer, block_expert, nused)
    y = moe_down(hdn, w_down, b_down, layer, block_expert, nused)
    return moe_combine_resid(x, y, pos, gates, gain, gate_mod, n_prompt, rows_per_sample)


def _rms(x, gain):
    return x * lax.rsqrt(jnp.mean(x * x, axis=-1, keepdims=True) + RMS_EPS) * gain


def _rope_tables(n_tokens, rot_dim):
    rows = n_tokens // GRID_W
    row = jnp.repeat(jnp.arange(rows, dtype=F32), GRID_W)
    col = jnp.tile(jnp.arange(GRID_W, dtype=F32), rows)
    axis_dim = rot_dim // 2
    inv_freq = ROPE_THETA ** (-jnp.arange(0, axis_dim, 2, dtype=F32) / axis_dim)
    ang = jnp.concatenate([row[:, None] * inv_freq, col[:, None] * inv_freq], axis=-1)
    cos, sin = jnp.cos(ang), jnp.sin(ang)
    return jnp.concatenate([cos, cos], axis=-1), jnp.concatenate([-sin, sin], axis=-1)


def _rope(x, c_tab, s_tab):
    shape = (c_tab.shape[0],) + (1,) * (x.ndim - 3) + (c_tab.shape[1],)
    return x * c_tab.reshape(shape) + jnp.roll(x, x.shape[-1] // 2, axis=-1) * s_tab.reshape(shape)


def _rope_sample_rows(x, tabs, n_p, bs, ts):
    xs = x[n_p:].reshape((bs, ts) + x.shape[1:])
    return jnp.concatenate([x[:n_p], _rope(xs, *tabs).reshape((bs * ts,) + x.shape[1:])], axis=0)


def _keys(bs, cache, own_sample, own_prompt):
    w = own_prompt.shape[-1]
    s = jnp.concatenate([cache.reshape(bs, -1, w), own_sample.reshape(bs, -1, w)], axis=1)
    return jnp.concatenate([s.reshape(-1, w), own_prompt], axis=0)


def _attend_groups(make_parts, *, dims, heads, dv, scale, out_dtype):
    bp, tp, bs, ts, past = dims
    q_parts, k_parts, v_part = make_parts
    o_p = attention(q_parts, k_parts, v_part, batch=bp, heads=heads, tq_total=tp, tk_total=tp,
                    q_row0=0, k_row0=bs * (past + ts), dv=dv, scale=scale, out_dtype=out_dtype)
    o_s = attention(q_parts, k_parts, v_part, batch=bs, heads=heads, tq_total=ts, tk_total=past + ts,
                    q_row0=bp * tp, k_row0=0, dv=dv, scale=scale, out_dtype=out_dtype)
    return jnp.concatenate([o_p, o_s], axis=0)


def even_mixer(h, dims, ropes, row_ropes, caches, w_in, q_norm, w_uq, kv_norm, w_ukv, lam_params, subln, w_out,
               lambda_init):
    bp, tp, bs, ts, past = dims
    n_p = bp * tp
    n = h.shape[0]
    c1 = MLA_Q_RANK + MLA_KV_RANK
    c2 = c1 + MLA_ROPE_DIM
    c3 = c2 + 2 * DIFF_QK_WIDTH
    p128, p64 = _deinterleave_matrix(LANE), _deinterleave_matrix(MLA_ROPE_DIM)
    w_in_bf = w_in.astype(BF16)
    a = mm(h, w_in_bf[:, :c1])
    w_kr = jnp.pad(w_in[:, c1:c2], ((0, 0), (0, LANE - MLA_ROPE_DIM)))
    kr = mm(h, permute_lanes(w_kr, p64), tn=LANE)[:, :MLA_ROPE_DIM]
    kr_own = mm(h, w_kr.astype(BF16), tn=LANE, rows=n_p)[:, :MLA_ROPE_DIM]
    dqk = mm(h, permute_lanes(w_in[:, c2:c3], p128))
    dqk_r = rope_cast(dqk, 0, 2 * DIFF_QK_WIDTH, *row_ropes[1])
    dk_own = mm(h, w_in_bf[:, c2 + DIFF_QK_WIDTH:c3], rows=n_p)
    dv_bf = mm(h, w_in_bf[:, c3:], out_dtype=BF16)
    dv_own = mm(h, w_in_bf[:, c3:], rows=n_p)
    cq, ckv = a[:, :MLA_Q_RANK], a[:, MLA_Q_RANK:]
    w_uq3 = w_uq.reshape(MLA_Q_RANK, MLA_HEADS, MLA_NOPE_DIM + MLA_ROPE_DIM)
    cq_n = _rms(cq, q_norm).astype(BF16)
    q_nope = mm(cq_n, w_uq3[:, :, :MLA_NOPE_DIM].reshape(MLA_Q_RANK, -1).astype(BF16), out_dtype=BF16)
    q_rope = mm(cq_n, permute_lanes(w_uq3[:, :, MLA_NOPE_DIM:].reshape(MLA_Q_RANK, -1), p64))
    q_rope = q_rope.reshape(n, MLA_HEADS, MLA_ROPE_DIM)
    ckv = _rms(ckv, kv_norm)

    q_rope = _rope_sample_rows(q_rope, ropes[0], n_p, bs, ts)
    kr_r = _rope_sample_rows(kr, ropes[0], n_p, bs, ts).astype(BF16)
    dk_r = dqk_r[:, DIFF_QK_WIDTH:]

    own = (ckv[:n_p].reshape(bp, tp, MLA_KV_RANK), kr_own.reshape(bp, tp, MLA_ROPE_DIM),
           dk_own.reshape(bp, tp, DIFF_HEADS, 2 * DIFF_HEAD_DIM),
           dv_own.reshape(bp, tp, DIFF_HEADS, DIFF_V_DIM))

    c_ckv, c_kr, c_dk, c_dv = caches
    ckv_bf = ckv.astype(BF16)
    ckv_keys = _keys(bs, c_ckv.astype(BF16), ckv_bf[n_p:], ckv_bf[:n_p])
    kv = mm(ckv_keys, w_ukv.astype(BF16), out_dtype=BF16)
    c_kr_perm = c_kr[..., _deinterleave_src(MLA_ROPE_DIM)[:MLA_ROPE_DIM]].astype(BF16)
    kr_keys = jnp.pad(_keys(bs, c_kr_perm, kr_r[n_p:], kr_r[:n_p]), ((0, 0), (0, LANE - MLA_ROPE_DIM)))
    q_rope_pad = jnp.pad(q_rope, ((0, 0), (0, 0), (0, LANE - MLA_ROPE_DIM))).reshape(n, -1).astype(BF16)
    a_out = _attend_groups(
        ([(q_nope, lambda hh: hh), (q_rope_pad, lambda hh: hh)],
         [(kv, lambda hh: 2 * hh), (kr_keys, lambda hh: 0)],
         (kv, lambda hh: 2 * hh + 1)),
        dims=dims, heads=MLA_HEADS, dv=MLA_V_DIM,
        scale=(MLA_NOPE_DIM + MLA_ROPE_DIM) ** -0.5, out_dtype=BF16)

    c_dk_perm = permute_lanes(c_dk.reshape(bs * past, -1), p128)
    dk_keys = _keys(bs, c_dk_perm, dk_r[n_p:], dk_r[:n_p])
    dv_keys = _keys(bs, c_dv.astype(BF16), dv_bf[n_p:], dv_bf[:n_p])
    o2 = _attend_groups(
        ([(dqk_r, lambda hh: hh)],
         [(dk_keys, lambda hh: hh)],
         (dv_keys, lambda hh: hh // 2)),
        dims=dims, heads=2 * DIFF_HEADS, dv=DIFF_V_DIM,
        scale=DIFF_HEAD_DIM ** -0.5, out_dtype=F32).reshape(n, DIFF_HEADS, 2 * DIFF_V_DIM)
    lq1, lk1, lq2, lk2 = lam_params.astype(F32)
    lam = jnp.exp(jnp.sum(lq1 * lk1)) - jnp.exp(jnp.sum(lq2 * lk2)) + lambda_init
    b_out = o2[:, :, :DIFF_V_DIM] - lam * o2[:, :, DIFF_V_DIM:]
    b_out = _rms(b_out, subln) * (1.0 - lambda_init)
    merged = jnp.concatenate([a_out, b_out.reshape(n, -1).astype(BF16)], axis=-1)
    return mm(merged, w_out.astype(BF16)), own


def odd_mixer(h, dims, ropes, row_ropes, caches, w_in, q_norm, k_norm, w_out):
    bp, tp, bs, ts, past = dims
    n_p = bp * tp
    n = h.shape[0]
    nq = GQA_Q_HEADS * GQA_HEAD_DIM
    nk = GQA_KV_HEADS * GQA_HEAD_DIM
    p128 = _deinterleave_matrix(LANE)
    src = _deinterleave_src(LANE)
    w_in_bf = w_in.astype(BF16)
    qk = mm(h, permute_lanes(w_in[:, :nq + nk], p128))
    k_own = _rms(mm(h, w_in_bf[:, nq:nq + nk], rows=n_p).reshape(n_p, GQA_KV_HEADS, GQA_HEAD_DIM), k_norm)
    q_r = rope_cast(qk, 0, nq, *row_ropes[2], gain=q_norm[src])
    k_r = rope_cast(qk, nq, nk, *row_ropes[2], gain=k_norm[src])
    v_bf = mm(h, w_in_bf[:, nq + nk:], out_dtype=BF16)
    v_own = mm(h, w_in_bf[:, nq + nk:], rows=n_p)
    own = (k_own.reshape(bp, tp, GQA_KV_HEADS, GQA_HEAD_DIM),
           v_own.reshape(bp, tp, GQA_KV_HEADS, GQA_HEAD_DIM))
    c_k, c_v = caches
    k_keys = _keys(bs, permute_lanes(c_k.reshape(bs * past, -1), p128), k_r[n_p:], k_r[:n_p])
    v_keys = _keys(bs, c_v.astype(BF16), v_bf[n_p:], v_bf[:n_p])
    rep = GQA_Q_HEADS // GQA_KV_HEADS
    o = _attend_groups(
        ([(q_r, lambda hh: hh)],
         [(k_keys, lambda hh: hh // rep)],
         (v_keys, lambda hh: hh // rep)),
        dims=dims, heads=GQA_Q_HEADS, dv=GQA_HEAD_DIM, scale=GQA_HEAD_DIM ** -0.5, out_dtype=BF16)
    return mm(o, w_out.astype(BF16)), own


def kernel(x_prompt, x_sample, cache_mla_ckv, cache_mla_krope, cache_diff_k, cache_diff_v, cache_gqa_k, cache_gqa_v, c, c_ctx, mod_w, mod_b, norm_gains, even_w_in, mla_q_norm, mla_w_uq, mla_kv_norm, mla_w_ukv, diff_lambda, diff_subln, even_w_out, odd_w_in, gqa_q_norm, gqa_k_norm, odd_w_out, router_w, router_b, moe_w_gu, moe_b_gu, moe_w_down, moe_b_down):
    bp, tp, d = x_prompt.shape
    bs, ts, _ = x_sample.shape
    past = cache_mla_ckv.shape[2]
    depth = mod_w.shape[0]
    n_p = bp * tp
    dims = (bp, tp, bs, ts, past)
    x = jnp.concatenate([x_prompt.reshape(n_p, d), x_sample.reshape(bs * ts, d)], axis=0)
    cond = jnp.concatenate([c_ctx[None, :], c], axis=0)
    n_seg = cond.shape[0]
    cond_act = jnp.pad(jax.nn.silu(cond), ((0, (-n_seg) % 16), (0, 0))).astype(BF16)
    ropes = (_rope_tables(ts, MLA_ROPE_DIM), _rope_tables(ts, DIFF_HEAD_DIM), _rope_tables(ts, GQA_HEAD_DIM))

    def per_row(tabs):
        c_tab, s_tab = tabs
        return (jnp.concatenate([jnp.ones((n_p, LANE), F32), jnp.tile(c_tab, (bs, 1))], axis=0),
                jnp.concatenate([jnp.zeros((n_p, LANE), F32), jnp.tile(s_tab, (bs, 1))], axis=0))

    row_ropes = (None, per_row(ropes[1]), per_row(ropes[2]))

    even_states, odd_states = [], []
    for layer in range(depth):
        mod = mm(cond_act, mod_w, w_index=layer)[:n_seg] + mod_b[layer]
        sh_m, sc_m, g_m, sh_f, sc_f, g_f = jnp.split(mod, 6, axis=-1)
        gains = norm_gains[layer]
        h = norm_mod(x, gains[0], sc_m, sh_m, n_p, ts, BF16)
        i = layer // 2
        if layer % 2 == 0:
            caches = (cache_mla_ckv[:, i], cache_mla_krope[:, i], cache_diff_k[:, i], cache_diff_v[:, i])
            lambda_init = 0.8 - 0.6 * math.exp(-0.3 * layer)
            m, own = even_mixer(h, dims, ropes, row_ropes, caches, even_w_in[i], mla_q_norm[i], mla_w_uq[i],
                                mla_kv_norm[i], mla_w_ukv[i], diff_lambda[i], diff_subln[i], even_w_out[i],
                                lambda_init)
            even_states.append(own)
        else:
            caches = (cache_gqa_k[:, i], cache_gqa_v[:, i])
            m, own = odd_mixer(h, dims, ropes, row_ropes, caches, odd_w_in[i], gqa_q_norm[i], gqa_k_norm[i], odd_w_out[i])
            odd_states.append(own)
        x, h = resid_norm_mod(x, m, gains[1], g_m, gains[2], sc_f, sh_f, n_p, ts)
        x = moe_block(x, h, router_w[layer], router_b[layer], moe_w_gu, moe_b_gu, moe_w_down, moe_b_down,
                      layer, gains[3], g_f, n_p, ts)

    y_prompt = x[:n_p].reshape(bp, tp, d)
    y_sample = x[n_p:].reshape(bs, ts, d)
    return (y_prompt, y_sample,
            jnp.stack([s[0] for s in even_states], axis=1),
            jnp.stack([s[1] for s in even_states], axis=1),
            jnp.stack([s[2] for s in even_states], axis=1),
            jnp.stack([s[3] for s in even_states], axis=1),
            jnp.stack([s[0] for s in odd_states], axis=1),
            jnp.stack([s[1] for s in odd_states], axis=1))
```
